```python
import math
import jax, jax.numpy as jnp
from jax import lax
import numpy as np

D_MODEL = 2048
BATCH = 4
SEQ = 2048
DEPTH = 4
DEC_BATCH = 8
DEC_SEQ = 1
PAST_LEN = 16384
PAGE_SIZE = 128

N_A_LAYERS = DEPTH // 2
N_B_LAYERS = DEPTH - N_A_LAYERS
N_HEADS = 16
HEAD_DIM = D_MODEL // N_HEADS
CONV_WIDTH = 31
D_FF = ((8 * D_MODEL // 3 + 255) // 256) * 256
MOBA_BLOCK = 256
MOBA_TOPK = 3
NUM_BUCKETS = 32
MAX_DISTANCE = 128
Q_CHUNK = 16
EPS = 1e-6
NEG = -1e30

kernel_name = "yoco_conformer_moba_step"


def rmsnorm(x, g):
    x32 = x.astype(jnp.float32)
    y = x32 * lax.rsqrt(jnp.mean(x32 * x32, axis=-1, keepdims=True) + EPS) * g.astype(jnp.float32)
    return y.astype(x.dtype)


def layernorm(x, g, b):
    x32 = x.astype(jnp.float32)
    mu = jnp.mean(x32, axis=-1, keepdims=True)
    xc = x32 - mu
    var = jnp.mean(xc * xc, axis=-1, keepdims=True)
    return (xc * lax.rsqrt(var + EPS) * g.astype(jnp.float32) + b.astype(jnp.float32)).astype(x.dtype)


def conv_module(h, left, w_in, b_in, w_dw, b_dw, ln_g, ln_b, w_out, b_out):
    a = h @ w_in + b_in
    u = a[..., :D_MODEL] * jax.nn.sigmoid(a[..., D_MODEL:])
    ext = jnp.concatenate([left.astype(u.dtype), u], axis=1)
    c = lax.conv_general_dilated(ext, w_dw[:, None, :], (1,), 'VALID',
                                 dimension_numbers=('NWC', 'WIO', 'NWC'),
                                 feature_group_count=D_MODEL) + b_dw
    c = layernorm(c, ln_g, ln_b)
    y = jax.nn.silu(c) @ w_out + b_out
    return y, ext[:, -(CONV_WIDTH - 1):]


def swiglu_ffn(h, w_gate, w_up, w_down):
    return (jax.nn.silu(h @ w_gate) * (h @ w_up)) @ w_down


def rel_bucket(dist):
    n = jnp.maximum(dist, 0)
    max_exact = NUM_BUCKETS // 2
    large = max_exact + (jnp.log(jnp.maximum(n, max_exact).astype(jnp.float32) / max_exact)
                         / math.log(MAX_DISTANCE / max_exact)
                         * (NUM_BUCKETS - max_exact)).astype(jnp.int32)
    large = jnp.minimum(large, NUM_BUCKETS - 1)
    return jnp.where(n < max_exact, n, large)


def joint_attend(l_shared, v_shared, l_sel=None, v_sel=None):
    m = jnp.max(l_shared, axis=-1)
    if l_sel is not None:
        m = jnp.maximum(m, jnp.max(l_sel, axis=-1))
    p = jnp.exp(l_shared - m[..., None])
    den = jnp.sum(p, axis=-1)
    o = jnp.einsum('bhqn,bhnd->bhqd', p, v_shared.astype(jnp.float32))
    if l_sel is not None:
        ps = jnp.exp(l_sel - m[..., None])
        den = den + jnp.sum(ps, axis=-1)
        o = o + jnp.einsum('bhqn,bhqnd->bhqd', ps, v_sel.astype(jnp.float32))
    return o / den[..., None]


def moba_prompt(q, k, v, rel_bias):
    B_, S, H, Dh = q.shape
    nb = -(-S // MOBA_BLOCK)
    pad = nb * MOBA_BLOCK - S
    kb = jnp.pad(k, ((0, 0), (0, pad), (0, 0), (0, 0))).reshape(B_, nb, MOBA_BLOCK, H, Dh).transpose(0, 3, 1, 2, 4)
    vb = jnp.pad(v, ((0, 0), (0, pad), (0, 0), (0, 0))).reshape(B_, nb, MOBA_BLOCK, H, Dh).transpose(0, 3, 1, 2, 4)
    k_mean = jnp.mean(kb, axis=3, dtype=jnp.float32)
    qt = q.transpose(0, 2, 1, 3)
    n_sel = min(MOBA_TOPK, nb)
    table_t = rel_bias.T.astype(jnp.float32)
    bi = jnp.arange(B_)[:, None, None, None]
    hi = jnp.arange(H)[None, :, None, None]
    offs = jnp.arange(MOBA_BLOCK)
    scale = HEAD_DIM ** -0.5

    def chunk(c):
        start = c * Q_CHUNK
        qc = lax.dynamic_slice_in_dim(qt, start, Q_CHUNK, axis=2)
        t = start + jnp.arange(Q_CHUNK)
        own = start // MOBA_BLOCK
        gate = jnp.einsum('bhqd,bhnd->bhqn', qc.astype(jnp.float32), k_mean)
        gate = jnp.where(jnp.arange(nb) < own, gate, NEG)
        _, sel = lax.top_k(gate, n_sel)
        valid = sel < own
        k_sel = kb[bi, hi, sel]
        v_sel = vb[bi, hi, sel]
        key_pos = sel[..., None] * MOBA_BLOCK + offs
        l_sel = (jnp.einsum('bhqd,bhqknd->bhqkn', qc, k_sel).astype(jnp.float32) * scale
                 + table_t[hi[..., None], rel_bucket(t[:, None, None] - key_pos)])
        l_sel = jnp.where(valid[..., None], l_sel, NEG)
        k_own = lax.dynamic_index_in_dim(kb, own, axis=2, keepdims=False)
        v_own = lax.dynamic_index_in_dim(vb, own, axis=2, keepdims=False)
        dist = t[:, None] - (own * MOBA_BLOCK + offs)[None, :]
        l_own = (jnp.einsum('bhqd,bhnd->bhqn', qc, k_own).astype(jnp.float32) * scale
                 + table_t[:, rel_bucket(dist)])
        l_own = jnp.where(dist >= 0, l_own, NEG)
        return joint_attend(l_own, v_own,
                            l_sel.reshape(B_, H, Q_CHUNK, n_sel * MOBA_BLOCK),
                            v_sel.reshape(B_, H, Q_CHUNK, n_sel * MOBA_BLOCK, Dh))

    out = lax.map(chunk, jnp.arange(S // Q_CHUNK))
    return out.transpose(1, 0, 3, 2, 4).reshape(B_, S, H, Dh)


def moba_sample(q, k_new, v_new, cache_k, cache_v, page_table, rel_bias):
    DB, Q, H, Dh = q.shape
    past_len = page_table.shape[1] * PAGE_SIZE
    ppb = MOBA_BLOCK // PAGE_SIZE
    nbp = past_len // MOBA_BLOCK
    r0 = nbp * MOBA_BLOCK
    tail = past_len - r0
    if tail > 0:
        tail_pages = page_table[:, r0 // PAGE_SIZE:]
        k_rec = jnp.concatenate([cache_k[tail_pages].reshape(DB, tail, H, Dh), k_new], axis=1)
        v_rec = jnp.concatenate([cache_v[tail_pages].reshape(DB, tail, H, Dh), v_new], axis=1)
    else:
        k_rec, v_rec = k_new, v_new
    nr = tail + Q
    nrb = -(-nr // MOBA_BLOCK)
    rec_mean = jnp.mean(jnp.pad(k_rec, ((0, 0), (0, nrb * MOBA_BLOCK - nr), (0, 0), (0, 0)))
                        .reshape(DB, nrb, MOBA_BLOCK, H, Dh), axis=2, dtype=jnp.float32).transpose(0, 2, 1, 3)
    t = past_len + jnp.arange(Q)
    own = t // MOBA_BLOCK
    table_t = rel_bias.T.astype(jnp.float32)
    scale = HEAD_DIM ** -0.5
    offs = jnp.arange(MOBA_BLOCK)
    qt = q.transpose(0, 2, 1, 3)
    bi = jnp.arange(DB)[:, None, None, None]
    hi = jnp.arange(H)[None, :, None, None]
    if nbp > 0:
        page_mean = jnp.mean(cache_k, axis=1, dtype=jnp.float32)
        pm = page_mean[page_table[:, :nbp * ppb]].reshape(DB, nbp, ppb, H, Dh).mean(axis=2)
        means = jnp.concatenate([pm.transpose(0, 2, 1, 3), rec_mean], axis=2)
    else:
        means = rec_mean
    nbt = nbp + nrb
    gate = jnp.einsum('bhqd,bhnd->bhqn', qt.astype(jnp.float32), means)
    gate = jnp.where(jnp.arange(nbt)[None, :] < own[:, None], gate, NEG)
    n_sel = min(MOBA_TOPK, nbt)
    _, sel = lax.top_k(gate, n_sel)
    valid = sel < own[:, None]
    rec_pos = r0 + jnp.arange(nr)
    rec_blk = rec_pos // MOBA_BLOCK
    sel_rec = jnp.any((sel[..., None] == rec_blk) & valid[..., None], axis=-2)
    vis = (rec_pos[None, :] <= t[:, None]) & ((rec_blk[None, :] == own[:, None]) | sel_rec)
    l_rec = (jnp.einsum('bhqd,bnhd->bhqn', qt, k_rec).astype(jnp.float32) * scale
             + table_t[:, rel_bucket(t[:, None] - rec_pos[None, :])])
    l_rec = jnp.where(vis, l_rec, NEG)
    v_rec_t = v_rec.transpose(0, 2, 1, 3)
    if nbp > 0:
        sel_p = jnp.minimum(sel, nbp - 1)
        valid_p = valid & (sel < nbp)
        logical = sel_p[..., None] * ppb + jnp.arange(ppb)
        phys = page_table[bi[..., None], logical]
        rows = jnp.arange(PAGE_SIZE)
        hi6 = hi[..., None, None]
        k_sel = cache_k[phys[..., None], rows, hi6].reshape(DB, H, Q, n_sel, MOBA_BLOCK, Dh)
        v_sel = cache_v[phys[..., None], rows, hi6].reshape(DB, H, Q, n_sel, MOBA_BLOCK, Dh)
        key_pos = sel_p[..., None] * MOBA_BLOCK + offs
        l_sel = (jnp.einsum('bhqd,bhqknd->bhqkn', qt, k_sel).astype(jnp.float32) * scale
                 + table_t[hi[..., None], rel_bucket(t[:, None, None] - key_pos)])
        l_sel = jnp.where(valid_p[..., None], l_sel, NEG)
        o = joint_attend(l_rec, v_rec_t,
                         l_sel.reshape(DB, H, Q, n_sel * MOBA_BLOCK),
                         v_sel.reshape(DB, H, Q, n_sel * MOBA_BLOCK, Dh))
    else:
        o = joint_attend(l_rec, v_rec_t)
    return o.transpose(0, 2, 1, 3)


def shared_kv(x, kv_norm_g, w_k, w_v):
    h = rmsnorm(x, kv_norm_g)
    N, T, _ = x.shape
    return (h @ w_k).reshape(N, T, N_HEADS, HEAD_DIM), (h @ w_v).reshape(N, T, N_HEADS, HEAD_DIM)


def setup_inputs(seed: int = 0) -> dict:
    key = jax.random.key(seed)
    ks = jax.random.split(key, 32)
    n_pages = PAST_LEN // PAGE_SIZE
    n_pool = (DEC_BATCH * n_pages * 5) // 4
    D, F, HD = D_MODEL, D_FF, N_HEADS * HEAD_DIM
    nrm = lambda k, shape, s: jax.random.normal(k, shape, jnp.float32) * s
    page_table = jax.random.permutation(ks[5], n_pool)[:DEC_BATCH * n_pages].reshape(DEC_BATCH, n_pages).astype(jnp.int32)
    return {
        "x_prompt": nrm(ks[0], (BATCH, SEQ, D), 1.0),
        "x_sample": nrm(ks[1], (DEC_BATCH, DEC_SEQ, D), 1.0),
        "state_conv": nrm(ks[2], (N_A_LAYERS, DEC_BATCH, CONV_WIDTH - 1, D), 0.5),
        "cache_k": nrm(ks[3], (n_pool, PAGE_SIZE, N_HEADS, HEAD_DIM), 1.0),
        "cache_v": nrm(ks[4], (n_pool, PAGE_SIZE, N_HEADS, HEAD_DIM), 1.0),
        "page_table": page_table,
        "norm_mix_g": 1.0 + nrm(ks[6], (DEPTH, D), 0.02),
        "norm_ffn_g": 1.0 + nrm(ks[7], (DEPTH, D), 0.02),
        "a_w_in": nrm(ks[8], (N_A_LAYERS, D, 2 * D), D ** -0.5),
        "a_b_in": nrm(ks[9], (N_A_LAYERS, 2 * D), 0.02),
        "a_w_dw": nrm(ks[10], (N_A_LAYERS, CONV_WIDTH, D), CONV_WIDTH ** -0.5),
        "a_b_dw": nrm(ks[11], (N_A_LAYERS, D), 0.02),
        "a_ln_g": 1.0 + nrm(ks[12], (N_A_LAYERS, D), 0.02),
        "a_ln_b": nrm(ks[13], (N_A_LAYERS, D), 0.02),
        "a_w_out": nrm(ks[14], (N_A_LAYERS, D, D), D ** -0.5),
        "a_b_out": nrm(ks[15], (N_A_LAYERS, D), 0.02),
        "kv_norm_g": 1.0 + nrm(ks[16], (D,), 0.02),
        "w_k": nrm(ks[17], (D, HD), D ** -0.5),
        "w_v": nrm(ks[18], (D, HD), D ** -0.5),
        "b_w_q": nrm(ks[19], (N_B_LAYERS, D, HD), D ** -0.5),
        "b_w_o": nrm(ks[20], (N_B_LAYERS, HD, D), HD ** -0.5),
        "rel_bias": nrm(ks[21], (NUM_BUCKETS, N_HEADS), 0.5),
        "f_w_gate": nrm(ks[22], (DEPTH, D, F), D ** -0.5),
        "f_w_up": nrm(ks[23], (DEPTH, D, F), D ** -0.5),
        "f_w_down": nrm(ks[24], (DEPTH, F, D), F ** -0.5),
        "final_norm_g": 1.0 + nrm(ks[25], (D,), 0.02),
    }


def reference(x_prompt, x_sample, state_conv, cache_k, cache_v, page_table,
              norm_mix_g, norm_ffn_g, a_w_in, a_b_in, a_w_dw, a_b_dw, a_ln_g, a_ln_b,
              a_w_out, a_b_out, kv_norm_g, w_k, w_v, b_w_q, b_w_o, rel_bias,
              f_w_gate, f_w_up, f_w_down, final_norm_g):
    xp, xs = x_prompt, x_sample
    left_p = jnp.zeros((xp.shape[0], CONV_WIDTH - 1, D_MODEL), xp.dtype)
    conv_p, conv_s = [], []
    kp = vp = ks_ = vs_ = None
    for l in range(DEPTH):
        if l < N_A_LAYERS:
            args = (a_w_in[l], a_b_in[l], a_w_dw[l], a_b_dw[l], a_ln_g[l], a_ln_b[l], a_w_out[l], a_b_out[l])
            yp, sp = conv_module(rmsnorm(xp, norm_mix_g[l]), left_p, *args)
            ys, ss = conv_module(rmsnorm(xs, norm_mix_g[l]), state_conv[l], *args)
            conv_p.append(sp)
            conv_s.append(ss)
        else:
            if l == N_A_LAYERS:
                kp, vp = shared_kv(xp, kv_norm_g, w_k, w_v)
                ks_, vs_ = shared_kv(xs, kv_norm_g, w_k, w_v)
            j = l - N_A_LAYERS
            Bp, Sp, _ = xp.shape
            Bs, Ss, _ = xs.shape
            qp = (rmsnorm(xp, norm_mix_g[l]) @ b_w_q[j]).reshape(Bp, Sp, N_HEADS, HEAD_DIM)
            qs = (rmsnorm(xs, norm_mix_g[l]) @ b_w_q[j]).reshape(Bs, Ss, N_HEADS, HEAD_DIM)
            op = moba_prompt(qp, kp, vp, rel_bias)
            os_ = moba_sample(qs, ks_, vs_, cache_k, cache_v, page_table, rel_bias)
            yp = op.reshape(Bp, Sp, N_HEADS * HEAD_DIM).astype(xp.dtype) @ b_w_o[j]
            ys = os_.reshape(Bs, Ss, N_HEADS * HEAD_DIM).astype(xs.dtype) @ b_w_o[j]
        xp = xp + yp
        xs = xs + ys
        xp = xp + swiglu_ffn(rmsnorm(xp, norm_ffn_g[l]), f_w_gate[l], f_w_up[l], f_w_down[l])
        xs = xs + swiglu_ffn(rmsnorm(xs, norm_ffn_g[l]), f_w_gate[l], f_w_up[l], f_w_down[l])
    y_prompt = rmsnorm(xp, final_norm_g)
    y_sample = rmsnorm(xs, final_norm_g)
    conv_state_p = jnp.stack(conv_p)
    conv_state_s = jnp.stack(conv_s)
    return (y_prompt, y_sample, conv_state_p, conv_state_s, kp, vp, ks_, vs_)
```

```python
import functools
import math

import numpy as np
import jax
import jax.numpy as jnp
from jax import lax
from jax.experimental import pallas as pl
from jax.experimental.pallas import tpu as pltpu

F32 = jnp.float32
BF16 = jnp.bfloat16

N_HEADS = 16
CONV_WIDTH = 31
MOBA_BLOCK = 256
MOBA_TOPK = 3
NUM_BUCKETS = 32
MAX_DISTANCE = 128
PAGE_SIZE = 128
EPS = 1e-6
NEG = -1e30

V7X_VMEM_BYTES = 64 * 1024 * 1024
V7X_LANES = 128
V7X_SUBLANES = 8
VMEM_CAP = V7X_VMEM_BYTES - 8 * 1024 * 1024


def _vmem_limit(est_bytes):
    return int(min(max(est_bytes + est_bytes // 4 + (4 << 20), 16 << 20), VMEM_CAP))


def _bucket_thresholds():
    max_exact = NUM_BUCKETS // 2
    n = np.arange(0, 4 * MAX_DISTANCE, dtype=np.int64)
    nf = np.maximum(n, max_exact).astype(np.float32)
    large = max_exact + (np.log(nf / np.float32(max_exact)) / np.float32(math.log(MAX_DISTANCE / max_exact))
                         * np.float32(NUM_BUCKETS - max_exact)).astype(np.int32)
    large = np.minimum(large, NUM_BUCKETS - 1)
    bucket = np.where(n < max_exact, n, large)
    assert np.all(np.diff(bucket) >= 0) and bucket[-1] == NUM_BUCKETS - 1
    return [int(np.argmax(bucket >= k)) for k in range(1, NUM_BUCKETS)]


_BUCKET_START = _bucket_thresholds()


def _bias_from_distance(dist, table_at):
    bias = jnp.full(dist.shape, table_at(0), F32)
    for k in range(1, NUM_BUCKETS):
        bias = jnp.where(dist >= _BUCKET_START[k - 1], table_at(k), bias)
    return bias


def _mm_body(*refs, has_norm, n_w, has_bias, has_res, epilogue, out_scale, n_out):
    it = iter(refs)
    x_ref = next(it)
    g_ref = next(it) if has_norm else None
    w_refs = [next(it) for _ in range(n_w)]
    b_refs = [next(it) for _ in range(n_w)] if has_bias else []
    res_ref = next(it) if has_res else None
    out_refs = [next(it) for _ in range(n_out)]
    if has_norm:
        xn_ref = next(it)

        @pl.when(pl.program_id(1) == 0)
        def _():
            x = x_ref[...]
            ms = jnp.mean(x * x, axis=-1, keepdims=True)
            xn_ref[...] = (x * lax.rsqrt(ms + EPS) * g_ref[...]).astype(xn_ref.dtype)

        xb = xn_ref[...]
    else:
        xb = x_ref[...]
    accs = [jnp.dot(xb, w[...], preferred_element_type=F32) for w in w_refs]
    if has_bias:
        accs = [a + b[...] for a, b in zip(accs, b_refs)]
    if epilogue == "glu":
        y = accs[0] * jax.nn.sigmoid(accs[1])
    elif epilogue == "swiglu":
        y = jax.nn.silu(accs[0]) * accs[1]
    else:
        y = accs[0]
    if out_scale is not None:
        y = y * out_scale
    if has_res:
        y = res_ref[...] + y
    for o in out_refs:
        o[...] = y.astype(o.dtype)


def fused_matmul(x, ws, n_cols, *, gain=None, biases=None, residual=None, epilogue="none",
                 out_dtypes=(F32,), out_scale=None, tm, tn, name):
    M, K = x.shape
    has_norm = gain is not None
    assert M % tm == 0 and n_cols % tn == 0
    grid = (M // tm, n_cols // tn)
    in_specs = [pl.BlockSpec((tm, K), lambda i, j: (i, 0))]
    args = [x]
    if has_norm:
        in_specs.append(pl.BlockSpec((1, K), lambda i, j: (0, 0)))
        args.append(gain.reshape(1, K))
    for w, off in ws:
        in_specs.append(pl.BlockSpec((K, tn), lambda i, j, off=off: (0, j + off)))
        args.append(w)
    if biases is not None:
        for b, off in biases:
            in_specs.append(pl.BlockSpec((1, tn), lambda i, j, off=off: (0, j + off)))
            args.append(b)
    if residual is not None:
        in_specs.append(pl.BlockSpec((tm, tn), lambda i, j: (i, j)))
        args.append(residual)
    out_specs = [pl.BlockSpec((tm, tn), lambda i, j: (i, j)) for _ in out_dtypes]
    out_shape = [jax.ShapeDtypeStruct((M, n_cols), dt) for dt in out_dtypes]
    scratch = [pltpu.VMEM((tm, K), BF16)] if has_norm else []
    est = (2 * tm * K * x.dtype.itemsize + (tm * K * 2 if has_norm else 0)
           + 2 * len(ws) * K * tn * 2 + 2 * tm * tn * 4 * (len(out_dtypes) + (residual is not None))
           + (len(ws) + 1) * tm * tn * 4)
    body = functools.partial(_mm_body, has_norm=has_norm, n_w=len(ws), has_bias=biases is not None,
                             has_res=residual is not None, epilogue=epilogue, out_scale=out_scale,
                             n_out=len(out_dtypes))
    outs = pl.pallas_call(
        body, grid=grid, in_specs=in_specs, out_specs=out_specs, out_shape=out_shape,
        scratch_shapes=scratch, name=name,
        compiler_params=pltpu.CompilerParams(dimension_semantics=("arbitrary", "arbitrary"),
                                             vmem_limit_bytes=_vmem_limit(est)),
    )(*args)
    return outs if len(outs) > 1 else outs[0]


def _rmsnorm_body(x_ref, g_ref, o_ref):
    x = x_ref[...]
    ms = jnp.mean(x * x, axis=-1, keepdims=True)
    o_ref[...] = x * lax.rsqrt(ms + EPS) * g_ref[...]


def rmsnorm_rows(x, gain, *, tm):
    M, D = x.shape
    return pl.pallas_call(
        _rmsnorm_body, grid=(M // tm,),
        in_specs=[pl.BlockSpec((tm, D), lambda i: (i, 0)), pl.BlockSpec((1, D), lambda i: (0, 0))],
        out_specs=pl.BlockSpec((tm, D), lambda i: (i, 0)),
        out_shape=jax.ShapeDtypeStruct((M, D), F32), name="rmsnorm_rows",
        compiler_params=pltpu.CompilerParams(dimension_semantics=("arbitrary",),
                                             vmem_limit_bytes=_vmem_limit(4 * tm * D * 4)),
    )(x, gain.reshape(1, D))


CONV_HALO = 32
CONV_ROWS = 64


def _ln_silu(c, g, b):
    mu = jnp.mean(c, axis=-1, keepdims=True)
    xc = c - mu
    var = jnp.mean(xc * xc, axis=-1, keepdims=True)
    y = xc * lax.rsqrt(var + EPS) * g + b
    return y * jax.nn.sigmoid(y)


def _conv_body(u_ref, halo_ref, left_ref, w_ref, bdw_ref, g_ref, b_ref, o_ref, ext_ref, y_ref, *, tt):
    first = pl.program_id(1) == 0
    ext_ref[0:CONV_HALO, :] = jnp.where(first, left_ref[...], halo_ref[...])
    ext_ref[CONV_HALO:, :] = u_ref[...]
    d_model = u_ref.shape[-1]
    lead = CONV_HALO - (CONV_WIDTH - 1)

    def chunk(c, carry):
        lanes = pl.ds(pl.multiple_of(c * V7X_LANES, V7X_LANES), V7X_LANES)
        w = w_ref[:, lanes]
        for r0 in range(0, tt, CONV_ROWS):
            acc = jnp.zeros((CONV_ROWS, V7X_LANES), F32)
            for k in range(CONV_WIDTH):
                acc = acc + ext_ref[r0 + lead + k: r0 + lead + k + CONV_ROWS, lanes] * w[k:k + 1, :]
            y_ref[r0:r0 + CONV_ROWS, lanes] = acc
        return carry

    lax.fori_loop(0, d_model // V7X_LANES, chunk, 0)
    c = y_ref[...] + bdw_ref[...]
    o_ref[...] = _ln_silu(c, g_ref[...], b_ref[...]).astype(o_ref.dtype)


def conv_ln_silu(u, left, w_dw, b_dw, ln_g, ln_b, *, tt):
    B, T, D = u.shape
    left_pad = jnp.pad(left, ((0, 0), (CONV_HALO - (CONV_WIDTH - 1), 0), (0, 0)))
    hb = tt // CONV_HALO
    row = lambda a: a.reshape(1, D)
    est = 2 * tt * D * 4 + 4 * CONV_HALO * D * 4 + (2 * tt + CONV_HALO) * D * 4 + 2 * tt * D * 2 + 6 * tt * D * 4
    return pl.pallas_call(
        functools.partial(_conv_body, tt=tt), grid=(B, T // tt),
        in_specs=[
            pl.BlockSpec((None, tt, D), lambda b, i: (b, i, 0)),
            pl.BlockSpec((None, CONV_HALO, D), lambda b, i: (b, jnp.maximum(i * hb - 1, 0), 0)),
            pl.BlockSpec((None, CONV_HALO, D), lambda b, i: (b, 0, 0)),
            pl.BlockSpec((CONV_WIDTH, D), lambda b, i: (0, 0)),
            pl.BlockSpec((1, D), lambda b, i: (0, 0)),
            pl.BlockSpec((1, D), lambda b, i: (0, 0)),
            pl.BlockSpec((1, D), lambda b, i: (0, 0)),
        ],
        out_specs=pl.BlockSpec((None, tt, D), lambda b, i: (b, i, 0)),
        out_shape=jax.ShapeDtypeStruct((B, T, D), BF16),
        scratch_shapes=[pltpu.VMEM((tt + CONV_HALO, D), F32), pltpu.VMEM((tt, D), F32)],
        name="conv_ln_silu",
        compiler_params=pltpu.CompilerParams(dimension_semantics=("arbitrary", "arbitrary"),
                                             vmem_limit_bytes=_vmem_limit(est)),
    )(u, u, left_pad, w_dw, row(b_dw), row(ln_g), row(ln_b))


def _conv_step_body(state_ref, u_ref, w_ref, bdw_ref, g_ref, b_ref, o_ref):
    acc = u_ref[...] * w_ref[CONV_WIDTH - 1:CONV_WIDTH, :]
    for k in range(CONV_WIDTH - 1):
        acc = acc + state_ref[:, k, :] * w_ref[k:k + 1, :]
    c = acc + bdw_ref[...]
    o_ref[...] = _ln_silu(c, g_ref[...], b_ref[...]).astype(o_ref.dtype)


def conv_step(state, u_new, w_dw, b_dw, ln_g, ln_b):
    NB, _, D = state.shape
    row = lambda a: a.reshape(1, D)
    return pl.pallas_call(
        _conv_step_body,
        out_shape=jax.ShapeDtypeStruct((NB, D), BF16), name="conv_step",
    )(state, u_new, w_dw, row(b_dw), row(ln_g), row(ln_b))


def _relbias_body(table_ref, o_ref):
    h = pl.program_id(0)
    a = lax.broadcasted_iota(jnp.int32, (MOBA_BLOCK, MOBA_BLOCK), 0)
    b = lax.broadcasted_iota(jnp.int32, (MOBA_BLOCK, MOBA_BLOCK), 1)
    far = table_ref[NUM_BUCKETS - 1, h]
    table_at = lambda k: table_ref[k, h]
    d_own = a - b
    own = _bias_from_distance(d_own, table_at) - far
    o_ref[0] = jnp.where(d_own >= 0, own, NEG)
    o_ref[1] = _bias_from_distance(d_own + MOBA_BLOCK, table_at) - far


def relbias_tables(rel_bias):
    return pl.pallas_call(
        _relbias_body, grid=(N_HEADS,),
        in_specs=[pl.BlockSpec(memory_space=pltpu.SMEM)],
        out_specs=pl.BlockSpec((None, 2, MOBA_BLOCK, MOBA_BLOCK), lambda h: (h, 0, 0, 0)),
        out_shape=jax.ShapeDtypeStruct((N_HEADS, 2, MOBA_BLOCK, MOBA_BLOCK), F32), name="relbias_tables",
    )(rel_bias)


GATE_ROWS = 16


def _softmax_block(s, v, m, l, acc):
    m_new = jnp.maximum(m, jnp.max(s, axis=-1, keepdims=True))
    alpha = jnp.exp(m - m_new)
    p = jnp.exp(s - m_new)
    l_new = alpha * l + jnp.sum(p, axis=-1, keepdims=True)
    acc_new = alpha * acc + jnp.dot(p.astype(BF16), v, preferred_element_type=F32)
    return m_new, l_new, acc_new


def _moba_prompt_body(q_ref, k_ref, v_ref, bias_ref, o_ref, kmean_ref, *, nb):
    i = pl.program_id(2)
    blk = MOBA_BLOCK
    hd = q_ref.shape[-1]

    @pl.when(i == 0)
    def _():
        r = lax.broadcasted_iota(jnp.int32, (GATE_ROWS, nb * blk), 0)
        c = lax.broadcasted_iota(jnp.int32, (GATE_ROWS, nb * blk), 1)
        avg = jnp.where(c // blk == r, 1.0 / blk, 0.0).astype(BF16)
        km = jnp.dot(avg, k_ref[...], preferred_element_type=F32)
        hi = km.astype(BF16)
        kmean_ref[0] = hi
        kmean_ref[1] = (km - hi.astype(F32)).astype(BF16)

    q = q_ref[...]
    nt = (((1,), (1,)), ((), ()))
    g2 = (lax.dot_general(kmean_ref[0], q, nt, preferred_element_type=F32)
          + lax.dot_general(kmean_ref[1], q, nt, preferred_element_type=F32))
    gate = g2[0:V7X_SUBLANES]
    row = lax.broadcasted_iota(jnp.int32, gate.shape, 0)
    gate = jnp.where(row < i, gate, NEG)
    beaten = jnp.zeros(gate.shape, jnp.int32)
    for jp in range(nb):
        gj = gate[jp:jp + 1, :]
        beaten = beaten + ((gj > gate) | ((gj == gate) & (jp < row))).astype(jnp.int32)
    chosen = (beaten < MOBA_TOPK) & (row < i)
    mask_t = jnp.where(chosen | (row == i), 0.0, NEG)
    mask_t = jnp.concatenate([mask_t, jnp.zeros((V7X_LANES - V7X_SUBLANES, blk), F32)], axis=0)
    qx = jnp.concatenate([q, mask_t.T.astype(BF16)], axis=1)
    lane = lax.broadcasted_iota(jnp.int32, (blk, V7X_LANES), 1)

    def scores(j, col):
        rows = pl.ds(pl.multiple_of(j * blk, blk), blk)
        kx = jnp.concatenate([k_ref[rows, :], (lane == col).astype(BF16)], axis=1)
        return lax.dot_general(qx, kx, nt, preferred_element_type=F32), v_ref[rows, :]

    s, v = scores(i, i)
    s = s + bias_ref[0]
    m = jnp.max(s, axis=-1, keepdims=True)
    p = jnp.exp(s - m)
    l = jnp.sum(p, axis=-1, keepdims=True)
    acc = jnp.dot(p.astype(BF16), v, preferred_element_type=F32)

    s, v = scores(jnp.maximum(i - 1, 0), (i + nb - 1) % nb)
    m, l, acc = _softmax_block(s + bias_ref[1], v, m, l, acc)

    def far(j, carry):
        s, v = scores(j, j)
        return _softmax_block(s, v, *carry)

    m, l, acc = lax.fori_loop(0, jnp.maximum(i - 1, 0), far, (m, l, acc))
    o_ref[...] = (acc / l).astype(o_ref.dtype)


def moba_prompt_attn(q, k, v, bias, *, n_seq, seq):
    M, HD = q.shape
    hd = HD // N_HEADS
    nb = seq // MOBA_BLOCK
    assert seq % MOBA_BLOCK == 0 and nb <= V7X_SUBLANES
    return pl.pallas_call(
        functools.partial(_moba_prompt_body, nb=nb), grid=(n_seq, N_HEADS, nb),
        in_specs=[
            pl.BlockSpec((MOBA_BLOCK, hd), lambda b, h, i: (b * nb + i, h)),
            pl.BlockSpec((seq, hd), lambda b, h, i: (b, h)),
            pl.BlockSpec((seq, hd), lambda b, h, i: (b, h)),
            pl.BlockSpec((None, 2, MOBA_BLOCK, MOBA_BLOCK), lambda b, h, i: (h, 0, 0, 0)),
        ],
        out_specs=pl.BlockSpec((MOBA_BLOCK, hd), lambda b, h, i: (b * nb + i, h)),
        out_shape=jax.ShapeDtypeStruct((M, HD), BF16),
        scratch_shapes=[pltpu.VMEM((2, GATE_ROWS, hd), BF16)], name="moba_prompt_attn",
        compiler_params=pltpu.CompilerParams(dimension_semantics=("arbitrary", "arbitrary", "arbitrary"),
                                             vmem_limit_bytes=_vmem_limit(8 << 20)),
    )(q, k, v, bias)


def _page_sums_body(pt_ref, page_ref, o_ref):
    s = jnp.sum(page_ref[...], axis=0)

    @pl.when(pl.program_id(2) == 0)
    def _():
        o_ref[...] = s

    @pl.when(pl.program_id(2) != 0)
    def _():
        o_ref[...] = o_ref[...] + s


def page_block_sums(cache_k, page_table, *, ppb):
    _, page, H, hd = cache_k.shape
    DB, n_pages = page_table.shape
    nblk = n_pages // ppb
    return pl.pallas_call(
        _page_sums_body,
        grid_spec=pltpu.PrefetchScalarGridSpec(
            num_scalar_prefetch=1, grid=(DB, nblk, ppb),
            in_specs=[pl.BlockSpec((None, page, H, hd), lambda b, j, p, pt: (pt[b, j * ppb + p], 0, 0, 0))],
            out_specs=pl.BlockSpec((None, None, H, hd), lambda b, j, p, pt: (b, j, 0, 0)),
        ),
        out_shape=jax.ShapeDtypeStruct((DB, nblk, H, hd), F32), name="page_block_sums",
        compiler_params=pltpu.CompilerParams(dimension_semantics=("arbitrary", "arbitrary", "arbitrary")),
    )(page_table, cache_k)


def _gate_topk_body(q_ref, sums_ref, knew_ref, o_ref, *, nbp, own, inv_rows):
    b = pl.program_id(0)
    HD = q_ref.shape[-1]
    hd = HD // N_HEADS
    q = q_ref[pl.ds(b, 1), :]
    r = lax.broadcasted_iota(jnp.int32, (HD, V7X_LANES), 0)
    c = lax.broadcasted_iota(jnp.int32, (HD, V7X_LANES), 1)
    ind = (r // hd == c).astype(BF16)

    def head_sums(prod):
        hi = prod.astype(BF16)
        mid = (prod - hi.astype(F32)).astype(BF16)
        lo = (prod - hi.astype(F32) - mid.astype(F32)).astype(BF16)
        return (jnp.dot(hi, ind, preferred_element_type=F32) + jnp.dot(mid, ind, preferred_element_type=F32)
                + jnp.dot(lo, ind, preferred_element_type=F32))

    means = sums_ref[...] * inv_rows
    gate_past = head_sums(means * q)
    k_new = jnp.broadcast_to(knew_ref[pl.ds(b, 1), :], (V7X_SUBLANES, HD))
    gate_rec = head_sums((k_new * inv_rows) * q)
    gate = jnp.concatenate([gate_past, gate_rec], axis=0)
    idx = lax.broadcasted_iota(jnp.int32, gate.shape, 0)
    gate = jnp.where(idx < own, gate, NEG)
    gate = jnp.where(idx <= nbp, gate, -jnp.inf)
    picks = []
    for _ in range(MOBA_TOPK):
        mx = jnp.max(gate, axis=0, keepdims=True)
        pick = jnp.min(jnp.where(gate == mx, idx, nbp + V7X_SUBLANES), axis=0, keepdims=True)
        picks.append(pick)
        gate = jnp.where(idx == pick, -jnp.inf, gate)
    picks.append(jnp.zeros((V7X_SUBLANES - MOBA_TOPK, V7X_LANES), jnp.int32))
    o_ref[...] = jnp.concatenate(picks, axis=0)


def moba_gate_topk(q_rows, block_sums, k_new_rows, *, n_seq, rows_per_block):
    R, HD = q_rows.shape
    _, nbp, _ = block_sums.shape
    return pl.pallas_call(
        functools.partial(_gate_topk_body, nbp=nbp, own=nbp, inv_rows=1.0 / rows_per_block), grid=(n_seq,),
        in_specs=[
            pl.BlockSpec((R, HD), lambda b: (0, 0)),
            pl.BlockSpec((None, nbp, HD), lambda b: (b, 0, 0)),
            pl.BlockSpec((R, HD), lambda b: (0, 0)),
        ],
        out_specs=pl.BlockSpec((None, V7X_SUBLANES, V7X_LANES), lambda b: (b, 0, 0)),
        out_shape=jax.ShapeDtypeStruct((n_seq, V7X_SUBLANES, V7X_LANES), jnp.int32), name="moba_gate_topk",
        compiler_params=pltpu.CompilerParams(dimension_semantics=("arbitrary",)),
    )(q_rows, block_sums, k_new_rows)


def _decode_attn_body(sel_ref, pt_ref, table_ref, q_ref, kn_ref, vn_ref, ck_ref, cv_ref, o_ref,
                      kbuf, vbuf, sem, *, nbp, ppb, t_new, scale):
    step = pl.program_id(0)
    hd = kbuf.shape[-1]

    def page_copies(st, slot):
        b, h = st // N_HEADS, st % N_HEADS
        out = []
        for kk in range(MOBA_TOPK):
            blk = jnp.minimum(sel_ref[b, h, kk], nbp - 1)
            for p in range(ppb):
                page = pt_ref[b, blk * ppb + p]
                i = kk * ppb + p
                out.append(pltpu.make_async_copy(ck_ref.at[page, :, h, :], kbuf.at[slot, i], sem.at[0, slot, i]))
                out.append(pltpu.make_async_copy(cv_ref.at[page, :, h, :], vbuf.at[slot, i], sem.at[1, slot, i]))
        return out

    slot = step % 2

    @pl.when(step == 0)
    def _():
        for c in page_copies(step, slot):
            c.start()

    @pl.when(step + 1 < pl.num_programs(0))
    def _():
        for c in page_copies(step + 1, 1 - slot):
            c.start()

    for c in page_copies(step, slot):
        c.wait()

    b, h = step // N_HEADS, step % N_HEADS
    lanes = pl.ds(pl.multiple_of(h * hd, hd), hd)
    def rows8(ref):
        x = ref[:, lanes]
        r = lax.broadcasted_iota(jnp.int32, x.shape, 0)
        row = jnp.sum(jnp.where(r == b, x, 0.0), axis=0, keepdims=True)
        return jnp.broadcast_to(row, (V7X_SUBLANES, hd))

    q = rows8(q_ref)
    qb = q.astype(BF16)
    nt = (((1,), (1,)), ((), ()))
    table_at = lambda k: table_ref[k, h]
    kn, vn = rows8(kn_ref), rows8(vn_ref)
    l_new = jnp.sum(q * kn, axis=-1, keepdims=True) * scale + table_at(0)
    logits, values = [], []
    for kk in range(MOBA_TOPK):
        sel = sel_ref[b, h, kk]
        valid = sel < nbp
        selp = jnp.minimum(sel, nbp - 1)
        for p in range(ppb):
            kp = kbuf[slot, kk * ppb + p].astype(BF16)
            s = lax.dot_general(qb, kp, nt, preferred_element_type=F32) * scale
            pos = selp * MOBA_BLOCK + p * PAGE_SIZE + lax.broadcasted_iota(jnp.int32, s.shape, 1)
            s = s + _bias_from_distance(t_new - pos, table_at)
            logits.append(jnp.where(valid, s, NEG))
            values.append(vbuf[slot, kk * ppb + p].astype(BF16))
    m = l_new
    for s in logits:
        m = jnp.maximum(m, jnp.max(s, axis=-1, keepdims=True))
    p_new = jnp.exp(l_new - m)
    den = p_new
    acc = p_new * vn
    for s, v in zip(logits, values):
        p = jnp.exp(s - m)
        den = den + jnp.sum(p, axis=-1, keepdims=True)
        acc = acc + jnp.dot(p.astype(BF16), v, preferred_element_type=F32)
    o_ref[...] = acc / den


def moba_decode_attn(sel, page_table, rel_bias, q_rows, k_new_rows, v_new_rows, cache_k, cache_v, *, n_seq):
    R, HD = q_rows.shape
    hd = HD // N_HEADS
    n_pages = page_table.shape[1]
    ppb = MOBA_BLOCK // PAGE_SIZE
    nbp = n_pages // ppb
    assert n_pages % ppb == 0, "past length must be a whole number of MoBA blocks"
    row_spec = pl.BlockSpec((R, HD), lambda s, sel, pt: (0, 0))
    n_buf = MOBA_TOPK * ppb
    body = functools.partial(_decode_attn_body, nbp=nbp, ppb=ppb, t_new=n_pages * PAGE_SIZE, scale=hd ** -0.5)
    return pl.pallas_call(
        body,
        grid_spec=pltpu.PrefetchScalarGridSpec(
            num_scalar_prefetch=2, grid=(n_seq * N_HEADS,),
            in_specs=[pl.BlockSpec(memory_space=pltpu.SMEM), row_spec, row_spec, row_spec,
                      pl.BlockSpec(memory_space=pl.ANY), pl.BlockSpec(memory_space=pl.ANY)],
            out_specs=pl.BlockSpec((None, V7X_SUBLANES, hd), lambda s, sel, pt: (s, 0, 0)),
            scratch_shapes=[pltpu.VMEM((2, n_buf, PAGE_SIZE, hd), F32), pltpu.VMEM((2, n_buf, PAGE_SIZE, hd), F32),
                            pltpu.SemaphoreType.DMA((2, 2, n_buf))],
        ),
        out_shape=jax.ShapeDtypeStruct((n_seq * N_HEADS, V7X_SUBLANES, hd), F32), name="moba_decode_attn",
        compiler_params=pltpu.CompilerParams(dimension_semantics=("arbitrary",)),
    )(sel, page_table, rel_bias, q_rows, k_new_rows, v_new_rows, cache_k, cache_v)


ROW_TILE = 512
COL_TILE = 1024
DEC_ROWS = 16
CONV_TIME_TILE = 256


def kernel(x_prompt, x_sample, state_conv, cache_k, cache_v, page_table, norm_mix_g, norm_ffn_g, a_w_in, a_b_in, a_w_dw, a_b_dw, a_ln_g, a_ln_b, a_w_out, a_b_out, kv_norm_g, w_k, w_v, b_w_q, b_w_o, rel_bias, f_w_gate, f_w_up, f_w_down, final_norm_g):
    B, S, D = x_prompt.shape
    DB, dec_seq, _ = x_sample.shape
    assert dec_seq == 1, "decode path handles one new token per sequence"
    n_a = a_w_in.shape[0]
    depth = norm_mix_g.shape[0]
    F = f_w_gate.shape[-1]
    n_pool = cache_k.shape[0]
    hd = D // N_HEADS
    ppb = MOBA_BLOCK // PAGE_SIZE
    n_pages = page_table.shape[1]

    bf = lambda w: w.astype(BF16)
    a_w_in_b, a_w_out_b = bf(a_w_in), bf(a_w_out)
    w_k_b, w_v_b, w_q_b, w_o_b = bf(w_k), bf(w_v), bf(b_w_q), bf(b_w_o)
    w_gate_b, w_up_b, w_down_b = bf(f_w_gate), bf(f_w_up), bf(f_w_down)

    xp = x_prompt.reshape(B * S, D)
    xs = jnp.pad(x_sample.reshape(DB, D), ((0, DEC_ROWS - DB), (0, 0)))
    streams = [dict(x=xp, tm=ROW_TILE, tag="p"), dict(x=xs, tm=DEC_ROWS, tag="s")]
    mm = lambda st, *a, **kw: fused_matmul(*a, tm=st["tm"], **kw)

    conv_p, conv_s = [], []
    bias_tables = relbias_tables(rel_bias)
    k_rows = v_rows = None
    block_sums = None
    for l in range(depth):
        for st in streams:
            x = st["x"]
            if l < n_a:
                u = mm(st, x, [(a_w_in_b[l], 0), (a_w_in_b[l], D // COL_TILE)], D, gain=norm_mix_g[l],
                       biases=[(a_b_in[l].reshape(1, 2 * D), 0), (a_b_in[l].reshape(1, 2 * D), D // COL_TILE)],
                       epilogue="glu", tn=COL_TILE, name="glu_" + st["tag"])
                if st["tag"] == "p":
                    u3 = u.reshape(B, S, D)
                    left = jnp.zeros((B, CONV_WIDTH - 1, D), F32)
                    c = conv_ln_silu(u3, left, a_w_dw[l], a_b_dw[l], a_ln_g[l], a_ln_b[l],
                                     tt=CONV_TIME_TILE).reshape(B * S, D)
                    conv_p.append(u3[:, S - (CONV_WIDTH - 1):, :])
                else:
                    c = conv_step(state_conv[l], u[:DB], a_w_dw[l], a_b_dw[l], a_ln_g[l], a_ln_b[l])
                    c = jnp.pad(c, ((0, DEC_ROWS - DB), (0, 0)))
                    conv_s.append(jnp.concatenate([state_conv[l][:, 1:, :], u[:DB, None, :]], axis=1))
                x = mm(st, c, [(a_w_out_b[l], 0)], D, biases=[(a_b_out[l].reshape(1, D), 0)], residual=x,
                       tn=COL_TILE, name="convout_" + st["tag"])
            else:
                j = l - n_a
                if l == n_a:
                    kf, kb = mm(st, x, [(w_k_b, 0)], D, gain=kv_norm_g, out_dtypes=(F32, BF16), tn=COL_TILE,
                                name="kproj_" + st["tag"])
                    vf, vb = mm(st, x, [(w_v_b, 0)], D, gain=kv_norm_g, out_dtypes=(F32, BF16), tn=COL_TILE,
                                name="vproj_" + st["tag"])
                    st["kv"] = (kf, vf, kb, vb)
                kf, vf, kb, vb = st["kv"]
                if st["tag"] == "p":
                    q = mm(st, x, [(w_q_b[j], 0)], D, gain=norm_mix_g[l], out_dtypes=(BF16,),
                           out_scale=hd ** -0.5, tn=COL_TILE, name="qproj_p")
                    o = moba_prompt_attn(q, kb, vb, bias_tables, n_seq=B, seq=S)
                else:
                    q = mm(st, x, [(w_q_b[j], 0)], D, gain=norm_mix_g[l], tn=COL_TILE, name="qproj_s")
                    if block_sums is None:
                        block_sums = page_block_sums(cache_k, page_table, ppb=ppb).reshape(DB, n_pages // ppb, D)
                    picks = moba_gate_topk(q, block_sums, kf, n_seq=DB, rows_per_block=MOBA_BLOCK)
                    sel = picks[:, :MOBA_TOPK, :N_HEADS].transpose(0, 2, 1)
                    o8 = moba_decode_attn(sel, page_table, rel_bias, q, kf, vf, cache_k, cache_v, n_seq=DB)
                    o = jnp.pad(o8[:, 0, :].reshape(DB, D), ((0, DEC_ROWS - DB), (0, 0))).astype(BF16)
                x = mm(st, o, [(w_o_b[j], 0)], D, residual=x, tn=COL_TILE, name="oproj_" + st["tag"])
            act = mm(st, x, [(w_gate_b[l], 0), (w_up_b[l], 0)], F, gain=norm_ffn_g[l], epilogue="swiglu",
                     out_dtypes=(BF16,), tn=F // 11, name="ffn_up_" + st["tag"])
            x = mm(st, act, [(w_down_b[l], 0)], D, residual=x, tn=COL_TILE // 2, name="ffn_down_" + st["tag"])
            st["x"] = x

    y_prompt = rmsnorm_rows(streams[0]["x"], final_norm_g, tm=ROW_TILE).reshape(B, S, D)
    y_sample = rmsnorm_rows(streams[1]["x"], final_norm_g, tm=DEC_ROWS)[:DB].reshape(DB, 1, D)
    kf_p, vf_p = streams[0]["kv"][:2]
    kf_s, vf_s = streams[1]["kv"][:2]
    return (y_prompt, y_sample, jnp.stack(conv_p), jnp.stack(conv_s),
            kf_p.reshape(B, S, N_HEADS, hd), vf_p.reshape(B, S, N_HEADS, hd),
            kf_s[:DB].reshape(DB, 1, N_HEADS, hd), vf_s[:DB].reshape(DB, 1, N_HEADS, hd))
```

```python
import functools
import math

import numpy as np
import jax
import jax.numpy as jnp
from jax import lax
from jax.experimental import pallas as pl
from jax.experimental.pallas import tpu as pltpu

F32 = jnp.float32
BF16 = jnp.bfloat16

N_HEADS = 16
CONV_WIDTH = 31
MOBA_BLOCK = 256
MOBA_TOPK = 3
NUM_BUCKETS = 32
MAX_DISTANCE = 128
PAGE_SIZE = 128
EPS = 1e-6
NEG = -1e30
LOG2E = math.log2(math.e)

V7X_VMEM_BYTES = 64 * 1024 * 1024
V7X_LANES = 128
V7X_SUBLANES = 8
VMEM_CAP = V7X_VMEM_BYTES - 8 * 1024 * 1024


def _vmem_limit(est_bytes):
    return int(min(max(est_bytes + est_bytes // 4 + (4 << 20), 16 << 20), VMEM_CAP))


def _bucket_thresholds():
    max_exact = NUM_BUCKETS // 2
    n = np.arange(0, 4 * MAX_DISTANCE, dtype=np.int64)
    nf = np.maximum(n, max_exact).astype(np.float32)
    large = max_exact + (np.log(nf / np.float32(max_exact)) / np.float32(math.log(MAX_DISTANCE / max_exact))
                         * np.float32(NUM_BUCKETS - max_exact)).astype(np.int32)
    large = np.minimum(large, NUM_BUCKETS - 1)
    bucket = np.where(n < max_exact, n, large)
    assert np.all(np.diff(bucket) >= 0) and bucket[-1] == NUM_BUCKETS - 1
    return [int(np.argmax(bucket >= k)) for k in range(1, NUM_BUCKETS)]


_BUCKET_START = _bucket_thresholds()


def _bias_from_distance(dist, table_at):
    bias = jnp.full(dist.shape, table_at(0), F32)
    for k in range(1, NUM_BUCKETS):
        bias = jnp.where(dist >= _BUCKET_START[k - 1], table_at(k), bias)
    return bias


def _mm_body(*refs, has_norm, n_w, has_bias, has_res, epilogue, out_scale, n_out):
    it = iter(refs)
    x_ref = next(it)
    g_ref = next(it) if has_norm else None
    w_refs = [next(it) for _ in range(n_w)]
    b_refs = [next(it) for _ in range(n_w)] if has_bias else []
    res_ref = next(it) if has_res else None
    out_refs = [next(it) for _ in range(n_out)]
    if has_norm:
        xn_ref = next(it)

        @pl.when(pl.program_id(1) == 0)
        def _():
            x = x_ref[...]
            ms = jnp.mean(x * x, axis=-1, keepdims=True)
            xn_ref[...] = (x * lax.rsqrt(ms + EPS) * g_ref[...]).astype(xn_ref.dtype)

        xb = xn_ref[...]
    else:
        xb = x_ref[...]
    accs = [jnp.dot(xb, w[...], preferred_element_type=F32) for w in w_refs]
    if has_bias:
        accs = [a + b[...] for a, b in zip(accs, b_refs)]
    if epilogue == "glu":
        y = accs[0] * jax.nn.sigmoid(accs[1])
    elif epilogue == "swiglu":
        y = jax.nn.silu(accs[0]) * accs[1]
    else:
        y = accs[0]
    if out_scale is not None:
        y = y * out_scale
    if has_res:
        y = res_ref[...] + y
    for o in out_refs:
        o[...] = y.astype(o.dtype)


def fused_matmul(x, ws, n_cols, *, gain=None, biases=None, residual=None, epilogue="none",
                 out_dtypes=(F32,), out_scale=None, tm, tn, name):
    M, K = x.shape
    has_norm = gain is not None
    assert M % tm == 0 and n_cols % tn == 0
    grid = (M // tm, n_cols // tn)
    in_specs = [pl.BlockSpec((tm, K), lambda i, j: (i, 0))]
    args = [x]
    if has_norm:
        in_specs.append(pl.BlockSpec((1, K), lambda i, j: (0, 0)))
        args.append(gain.reshape(1, K))
    for w, off in ws:
        in_specs.append(pl.BlockSpec((K, tn), lambda i, j, off=off: (0, j + off)))
        args.append(w)
    if biases is not None:
        for b, off in biases:
            in_specs.append(pl.BlockSpec((1, tn), lambda i, j, off=off: (0, j + off)))
            args.append(b)
    if residual is not None:
        in_specs.append(pl.BlockSpec((tm, tn), lambda i, j: (i, j)))
        args.append(residual)
    out_specs = [pl.BlockSpec((tm, tn), lambda i, j: (i, j)) for _ in out_dtypes]
    out_shape = [jax.ShapeDtypeStruct((M, n_cols), dt) for dt in out_dtypes]
    scratch = [pltpu.VMEM((tm, K), BF16)] if has_norm else []
    est = (2 * tm * K * x.dtype.itemsize + (tm * K * 2 if has_norm else 0)
           + 2 * len(ws) * K * tn * 2 + 2 * tm * tn * 4 * (len(out_dtypes) + (residual is not None))
           + (len(ws) + 1) * tm * tn * 4)
    body = functools.partial(_mm_body, has_norm=has_norm, n_w=len(ws), has_bias=biases is not None,
                             has_res=residual is not None, epilogue=epilogue, out_scale=out_scale,
                             n_out=len(out_dtypes))
    outs = pl.pallas_call(
        body, grid=grid, in_specs=in_specs, out_specs=out_specs, out_shape=out_shape,
        scratch_shapes=scratch, name=name,
        compiler_params=pltpu.CompilerParams(dimension_semantics=("arbitrary", "arbitrary"),
                                             vmem_limit_bytes=_vmem_limit(est)),
    )(*args)
    return outs if len(outs) > 1 else outs[0]


def _rmsnorm_body(x_ref, g_ref, o_ref):
    x = x_ref[...]
    ms = jnp.mean(x * x, axis=-1, keepdims=True)
    o_ref[...] = x * lax.rsqrt(ms + EPS) * g_ref[...]


def rmsnorm_rows(x, gain, *, tm):
    M, D = x.shape
    return pl.pallas_call(
        _rmsnorm_body, grid=(M // tm,),
        in_specs=[pl.BlockSpec((tm, D), lambda i: (i, 0)), pl.BlockSpec((1, D), lambda i: (0, 0))],
        out_specs=pl.BlockSpec((tm, D), lambda i: (i, 0)),
        out_shape=jax.ShapeDtypeStruct((M, D), F32), name="rmsnorm_rows",
        compiler_params=pltpu.CompilerParams(dimension_semantics=("arbitrary",),
                                             vmem_limit_bytes=_vmem_limit(4 * tm * D * 4)),
    )(x, gain.reshape(1, D))


CONV_HALO = 32
CONV_ROWS = 64


def _ln_silu(c, g, b):
    mu = jnp.mean(c, axis=-1, keepdims=True)
    xc = c - mu
    var = jnp.mean(xc * xc, axis=-1, keepdims=True)
    y = xc * lax.rsqrt(var + EPS) * g + b
    return y * jax.nn.sigmoid(y)


def _conv_body(u_ref, halo_ref, left_ref, w_ref, bdw_ref, g_ref, b_ref, o_ref, ext_ref, y_ref, *, tt):
    first = pl.program_id(1) == 0
    ext_ref[0:CONV_HALO, :] = jnp.where(first, left_ref[...], halo_ref[...])
    ext_ref[CONV_HALO:, :] = u_ref[...]
    d_model = u_ref.shape[-1]
    lead = CONV_HALO - (CONV_WIDTH - 1)

    def chunk(c, carry):
        lanes = pl.ds(pl.multiple_of(c * V7X_LANES, V7X_LANES), V7X_LANES)
        w = w_ref[:, lanes]
        for r0 in range(0, tt, CONV_ROWS):
            n_win = CONV_ROWS + CONV_HALO
            win = ext_ref[r0:r0 + n_win, lanes]
            acc = jnp.zeros((CONV_ROWS, V7X_LANES), F32)
            for res in range(V7X_SUBLANES):
                shifted = win if res == 0 else pltpu.roll(win, n_win - res, axis=0)
                for off in range(res, lead + CONV_WIDTH, V7X_SUBLANES):
                    k = off - lead
                    if k >= 0:
                        acc = acc + shifted[off - res:off - res + CONV_ROWS, :] * w[k:k + 1, :]
            y_ref[r0:r0 + CONV_ROWS, lanes] = acc
        return carry

    lax.fori_loop(0, d_model // V7X_LANES, chunk, 0)
    c = y_ref[...] + bdw_ref[...]
    o_ref[...] = _ln_silu(c, g_ref[...], b_ref[...]).astype(o_ref.dtype)


def conv_ln_silu(u, left, w_dw, b_dw, ln_g, ln_b, *, tt):
    B, T, D = u.shape
    left_pad = jnp.pad(left, ((0, 0), (CONV_HALO - (CONV_WIDTH - 1), 0), (0, 0)))
    hb = tt // CONV_HALO
    row = lambda a: a.reshape(1, D)
    est = 2 * tt * D * 4 + 4 * CONV_HALO * D * 4 + (2 * tt + CONV_HALO) * D * 4 + 2 * tt * D * 2 + 6 * tt * D * 4
    return pl.pallas_call(
        functools.partial(_conv_body, tt=tt), grid=(B, T // tt),
        in_specs=[
            pl.BlockSpec((None, tt, D), lambda b, i: (b, i, 0)),
            pl.BlockSpec((None, CONV_HALO, D), lambda b, i: (b, jnp.maximum(i * hb - 1, 0), 0)),
            pl.BlockSpec((None, CONV_HALO, D), lambda b, i: (b, 0, 0)),
            pl.BlockSpec((CONV_WIDTH, D), lambda b, i: (0, 0)),
            pl.BlockSpec((1, D), lambda b, i: (0, 0)),
            pl.BlockSpec((1, D), lambda b, i: (0, 0)),
            pl.BlockSpec((1, D), lambda b, i: (0, 0)),
        ],
        out_specs=pl.BlockSpec((None, tt, D), lambda b, i: (b, i, 0)),
        out_shape=jax.ShapeDtypeStruct((B, T, D), BF16),
        scratch_shapes=[pltpu.VMEM((tt + CONV_HALO, D), F32), pltpu.VMEM((tt, D), F32)],
        name="conv_ln_silu",
        compiler_params=pltpu.CompilerParams(dimension_semantics=("arbitrary", "arbitrary"),
                                             vmem_limit_bytes=_vmem_limit(est)),
    )(u, u, left_pad, w_dw, row(b_dw), row(ln_g), row(ln_b))


def _conv_step_body(state_ref, u_ref, w_ref, bdw_ref, g_ref, b_ref, o_ref):
    acc = u_ref[...] * w_ref[CONV_WIDTH - 1:CONV_WIDTH, :]
    for k in range(CONV_WIDTH - 1):
        acc = acc + state_ref[:, k, :] * w_ref[k:k + 1, :]
    c = acc + bdw_ref[...]
    o_ref[...] = _ln_silu(c, g_ref[...], b_ref[...]).astype(o_ref.dtype)


def conv_step(state, u_new, w_dw, b_dw, ln_g, ln_b):
    NB, _, D = state.shape
    row = lambda a: a.reshape(1, D)
    return pl.pallas_call(
        _conv_step_body,
        out_shape=jax.ShapeDtypeStruct((NB, D), BF16), name="conv_step",
    )(state, u_new, w_dw, row(b_dw), row(ln_g), row(ln_b))


def _relbias_body(table_ref, o_ref):
    h = pl.program_id(0)
    a = lax.broadcasted_iota(jnp.int32, (MOBA_BLOCK, MOBA_BLOCK), 0)
    b = lax.broadcasted_iota(jnp.int32, (MOBA_BLOCK, MOBA_BLOCK), 1)
    far = table_ref[NUM_BUCKETS - 1, h]
    table_at = lambda k: table_ref[k, h]
    d_own = a - b
    own = (_bias_from_distance(d_own, table_at) - far) * LOG2E
    o_ref[0] = jnp.where(d_own >= 0, own, NEG)
    o_ref[1] = (_bias_from_distance(d_own + MOBA_BLOCK, table_at) - far) * LOG2E


def relbias_tables(rel_bias):
    return pl.pallas_call(
        _relbias_body, grid=(N_HEADS,),
        in_specs=[pl.BlockSpec(memory_space=pltpu.SMEM)],
        out_specs=pl.BlockSpec((None, 2, MOBA_BLOCK, MOBA_BLOCK), lambda h: (h, 0, 0, 0)),
        out_shape=jax.ShapeDtypeStruct((N_HEADS, 2, MOBA_BLOCK, MOBA_BLOCK), F32), name="relbias_tables",
    )(rel_bias)


GATE_ROWS = 16


def _moba_prompt_body(q_ref, k_ref, v_ref, bias_ref, o_ref, kx_ref, kmean_ref, *, nb):
    i = pl.program_id(2)
    blk = MOBA_BLOCK
    hd = q_ref.shape[-1]
    seq = nb * blk

    @pl.when(i == 0)
    def _():
        kx_ref[:, 0:hd] = k_ref[...]
        kr = lax.broadcasted_iota(jnp.int32, (seq, V7X_LANES), 0)
        kc = lax.broadcasted_iota(jnp.int32, (seq, V7X_LANES), 1)
        kx_ref[:, hd:] = (kr // blk == kc).astype(BF16)
        r = lax.broadcasted_iota(jnp.int32, (GATE_ROWS, seq), 0)
        c = lax.broadcasted_iota(jnp.int32, (GATE_ROWS, seq), 1)
        avg = jnp.where(c // blk == r, 1.0 / blk, 0.0).astype(BF16)
        km = jnp.dot(avg, k_ref[...], preferred_element_type=F32)
        hi = km.astype(BF16)
        kmean_ref[0] = hi
        kmean_ref[1] = (km - hi.astype(F32)).astype(BF16)

    q = q_ref[...]
    nt = (((1,), (1,)), ((), ()))
    g2 = (lax.dot_general(kmean_ref[0], q, nt, preferred_element_type=F32)
          + lax.dot_general(kmean_ref[1], q, nt, preferred_element_type=F32))
    gate = g2[0:V7X_SUBLANES]
    row = lax.broadcasted_iota(jnp.int32, gate.shape, 0)
    gate = jnp.where(row < i, gate, NEG)
    beaten = jnp.zeros(gate.shape, jnp.int32)
    for jp in range(nb):
        gj = gate[jp:jp + 1, :]
        beaten = beaten + ((gj > gate) | ((gj == gate) & (jp < row))).astype(jnp.int32)
    chosen = (beaten < MOBA_TOPK) & (row < i)
    mask_t = jnp.where(chosen | (row == i), 0.0, NEG)
    mask_t = jnp.concatenate([mask_t, jnp.zeros((V7X_LANES - V7X_SUBLANES, blk), F32)], axis=0)
    qx = jnp.concatenate([q, mask_t.T.astype(BF16)], axis=1)

    for own in range(nb):
        @pl.when(i == own)
        def _(own=own):
            n = (own + 1) * blk
            s = lax.dot_general(qx, kx_ref[0:n, :], nt, preferred_element_type=F32)
            parts = [s[:, n - blk:] + bias_ref[0]]
            if own >= 1:
                parts.insert(0, s[:, n - 2 * blk:n - blk] + bias_ref[1])
            if own >= 2:
                parts.insert(0, s[:, :n - 2 * blk])
            s = jnp.concatenate(parts, axis=1) if len(parts) > 1 else parts[0]
            m = jnp.max(s, axis=-1, keepdims=True)
            p = jnp.exp2(s - m)
            l = jnp.sum(p, axis=-1, keepdims=True)
            pv = jnp.dot(p.astype(BF16), v_ref[0:n, :], preferred_element_type=F32)
            o_ref[...] = (pv * (1.0 / l)).astype(o_ref.dtype)


def moba_prompt_attn(q, k, v, bias, *, n_seq, seq):
    M, HD = q.shape
    hd = HD // N_HEADS
    nb = seq // MOBA_BLOCK
    assert seq % MOBA_BLOCK == 0 and nb <= V7X_SUBLANES
    return pl.pallas_call(
        functools.partial(_moba_prompt_body, nb=nb), grid=(n_seq, N_HEADS, nb),
        in_specs=[
            pl.BlockSpec((MOBA_BLOCK, hd), lambda b, h, i: (b * nb + i, h)),
            pl.BlockSpec((seq, hd), lambda b, h, i: (b, h)),
            pl.BlockSpec((seq, hd), lambda b, h, i: (b, h)),
            pl.BlockSpec((None, 2, MOBA_BLOCK, MOBA_BLOCK), lambda b, h, i: (h, 0, 0, 0)),
        ],
        out_specs=pl.BlockSpec((MOBA_BLOCK, hd), lambda b, h, i: (b * nb + i, h)),
        out_shape=jax.ShapeDtypeStruct((M, HD), BF16),
        scratch_shapes=[pltpu.VMEM((seq, hd + V7X_LANES), BF16), pltpu.VMEM((2, GATE_ROWS, hd), BF16)],
        name="moba_prompt_attn",
        compiler_params=pltpu.CompilerParams(dimension_semantics=("arbitrary", "arbitrary", "arbitrary"),
                                             vmem_limit_bytes=_vmem_limit(8 * MOBA_BLOCK * seq * 4)),
    )(q, k, v, bias)


PAGES_PER_STEP = 8


def _page_sums_body(pt_ref, *refs, ppb):
    page_refs, o_ref = refs[:-1], refs[-1]
    for j in range(len(page_refs) // ppb):
        tot = jnp.sum(page_refs[j * ppb][...], axis=0)
        for p in range(1, ppb):
            tot = tot + jnp.sum(page_refs[j * ppb + p][...], axis=0)
        o_ref[j] = tot


def page_block_sums(cache_k, page_table, *, ppb):
    _, page, H, hd = cache_k.shape
    DB, n_pages = page_table.shape
    nblk = n_pages // ppb
    pps = PAGES_PER_STEP
    assert pps % ppb == 0 and n_pages % pps == 0

    def page_spec(k):
        return pl.BlockSpec((None, page, H, hd), lambda b, j, pt: (pt[b, j * pps + k], 0, 0, 0))

    return pl.pallas_call(
        functools.partial(_page_sums_body, ppb=ppb),
        grid_spec=pltpu.PrefetchScalarGridSpec(
            num_scalar_prefetch=1, grid=(DB, n_pages // pps),
            in_specs=[page_spec(k) for k in range(pps)],
            out_specs=pl.BlockSpec((None, pps // ppb, H, hd), lambda b, j, pt: (b, j, 0, 0)),
        ),
        out_shape=jax.ShapeDtypeStruct((DB, nblk, H, hd), F32), name="page_block_sums",
        compiler_params=pltpu.CompilerParams(
            dimension_semantics=("arbitrary", "arbitrary"),
            vmem_limit_bytes=_vmem_limit(2 * pps * page * H * hd * 4)),
    )(page_table, *([cache_k] * pps))


def _gate_topk_body(q_ref, sums_ref, knew_ref, o_ref, *, nbp, own, inv_rows):
    b = pl.program_id(0)
    HD = q_ref.shape[-1]
    hd = HD // N_HEADS
    q = q_ref[pl.ds(b, 1), :]
    r = lax.broadcasted_iota(jnp.int32, (HD, V7X_LANES), 0)
    c = lax.broadcasted_iota(jnp.int32, (HD, V7X_LANES), 1)
    ind = (r // hd == c).astype(BF16)

    def head_sums(prod):
        hi = prod.astype(BF16)
        mid = (prod - hi.astype(F32)).astype(BF16)
        lo = (prod - hi.astype(F32) - mid.astype(F32)).astype(BF16)
        return (jnp.dot(hi, ind, preferred_element_type=F32) + jnp.dot(mid, ind, preferred_element_type=F32)
                + jnp.dot(lo, ind, preferred_element_type=F32))

    means = sums_ref[...] * inv_rows
    gate_past = head_sums(means * q)
    k_new = jnp.broadcast_to(knew_ref[pl.ds(b, 1), :], (V7X_SUBLANES, HD))
    gate_rec = head_sums((k_new * inv_rows) * q)
    gate = jnp.concatenate([gate_past, gate_rec], axis=0)
    idx = lax.broadcasted_iota(jnp.int32, gate.shape, 0)
    gate = jnp.where(idx < own, gate, NEG)
    gate = jnp.where(idx <= nbp, gate, -jnp.inf)
    picks = []
    for _ in range(MOBA_TOPK):
        mx = jnp.max(gate, axis=0, keepdims=True)
        pick = jnp.min(jnp.where(gate == mx, idx, nbp + V7X_SUBLANES), axis=0, keepdims=True)
        picks.append(pick)
        gate = jnp.where(idx == pick, -jnp.inf, gate)
    picks.append(jnp.zeros((V7X_SUBLANES - MOBA_TOPK, V7X_LANES), jnp.int32))
    o_ref[...] = jnp.concatenate(picks, axis=0)


def moba_gate_topk(q_rows, block_sums, k_new_rows, *, n_seq, rows_per_block):
    R, HD = q_rows.shape
    _, nbp, _ = block_sums.shape
    return pl.pallas_call(
        functools.partial(_gate_topk_body, nbp=nbp, own=nbp, inv_rows=1.0 / rows_per_block), grid=(n_seq,),
        in_specs=[
            pl.BlockSpec((R, HD), lambda b: (0, 0)),
            pl.BlockSpec((None, nbp, HD), lambda b: (b, 0, 0)),
            pl.BlockSpec((R, HD), lambda b: (0, 0)),
        ],
        out_specs=pl.BlockSpec((None, V7X_SUBLANES, V7X_LANES), lambda b: (b, 0, 0)),
        out_shape=jax.ShapeDtypeStruct((n_seq, V7X_SUBLANES, V7X_LANES), jnp.int32), name="moba_gate_topk",
        compiler_params=pltpu.CompilerParams(dimension_semantics=("arbitrary",)),
    )(q_rows, block_sums, k_new_rows)


def _decode_attn_body(sel_ref, pt_ref, table_ref, q_ref, kn_ref, vn_ref, ck_ref, cv_ref, o_ref,
                      kbuf, vbuf, sem, *, nbp, ppb, t_new, scale):
    step = pl.program_id(0)
    hd = kbuf.shape[-1]

    def page_copies(st, slot):
        b, h = st // N_HEADS, st % N_HEADS
        out = []
        for kk in range(MOBA_TOPK):
            blk = jnp.minimum(sel_ref[b, h, kk], nbp - 1)
            for p in range(ppb):
                page = pt_ref[b, blk * ppb + p]
                i = kk * ppb + p
                out.append(pltpu.make_async_copy(ck_ref.at[page, :, h, :], kbuf.at[slot, i], sem.at[0, slot, i]))
                out.append(pltpu.make_async_copy(cv_ref.at[page, :, h, :], vbuf.at[slot, i], sem.at[1, slot, i]))
        return out

    slot = step % 2

    @pl.when(step == 0)
    def _():
        for c in page_copies(step, slot):
            c.start()

    @pl.when(step + 1 < pl.num_programs(0))
    def _():
        for c in page_copies(step + 1, 1 - slot):
            c.start()

    for c in page_copies(step, slot):
        c.wait()

    b, h = step // N_HEADS, step % N_HEADS
    lanes = pl.ds(pl.multiple_of(h * hd, hd), hd)
    def rows8(ref):
        x = ref[:, lanes]
        r = lax.broadcasted_iota(jnp.int32, x.shape, 0)
        row = jnp.sum(jnp.where(r == b, x, 0.0), axis=0, keepdims=True)
        return jnp.broadcast_to(row, (V7X_SUBLANES, hd))

    q = rows8(q_ref)
    qb = q.astype(BF16)
    nt = (((1,), (1,)), ((), ()))
    table_at = lambda k: table_ref[k, h]
    kn, vn = rows8(kn_ref), rows8(vn_ref)
    l_new = jnp.sum(q * kn, axis=-1, keepdims=True) * scale + table_at(0)
    logits, values = [], []
    for kk in range(MOBA_TOPK):
        sel = sel_ref[b, h, kk]
        valid = sel < nbp
        selp = jnp.minimum(sel, nbp - 1)
        for p in range(ppb):
            kp = kbuf[slot, kk * ppb + p].astype(BF16)
            s = lax.dot_general(qb, kp, nt, preferred_element_type=F32) * scale
            pos = selp * MOBA_BLOCK + p * PAGE_SIZE + lax.broadcasted_iota(jnp.int32, s.shape, 1)
            s = s + _bias_from_distance(t_new - pos, table_at)
            logits.append(jnp.where(valid, s, NEG))
            values.append(vbuf[slot, kk * ppb + p].astype(BF16))
    m = l_new
    for s in logits:
        m = jnp.maximum(m, jnp.max(s, axis=-1, keepdims=True))
    p_new = jnp.exp(l_new - m)
    den = p_new
    acc = p_new * vn
    for s, v in zip(logits, values):
        p = jnp.exp(s - m)
        den = den + jnp.sum(p, axis=-1, keepdims=True)
        acc = acc + jnp.dot(p.astype(BF16), v, preferred_element_type=F32)
    o_ref[...] = acc / den


def moba_decode_attn(sel, page_table, rel_bias, q_rows, k_new_rows, v_new_rows, cache_k, cache_v, *, n_seq):
    R, HD = q_rows.shape
    hd = HD // N_HEADS
    n_pages = page_table.shape[1]
    ppb = MOBA_BLOCK // PAGE_SIZE
    nbp = n_pages // ppb
    assert n_pages % ppb == 0, "past length must be a whole number of MoBA blocks"
    row_spec = pl.BlockSpec((R, HD), lambda s, sel, pt: (0, 0))
    n_buf = MOBA_TOPK * ppb
    body = functools.partial(_decode_attn_body, nbp=nbp, ppb=ppb, t_new=n_pages * PAGE_SIZE, scale=hd ** -0.5)
    return pl.pallas_call(
        body,
        grid_spec=pltpu.PrefetchScalarGridSpec(
            num_scalar_prefetch=2, grid=(n_seq * N_HEADS,),
            in_specs=[pl.BlockSpec(memory_space=pltpu.SMEM), row_spec, row_spec, row_spec,
                      pl.BlockSpec(memory_space=pl.ANY), pl.BlockSpec(memory_space=pl.ANY)],
            out_specs=pl.BlockSpec((None, V7X_SUBLANES, hd), lambda s, sel, pt: (s, 0, 0)),
            scratch_shapes=[pltpu.VMEM((2, n_buf, PAGE_SIZE, hd), F32), pltpu.VMEM((2, n_buf, PAGE_SIZE, hd), F32),
                            pltpu.SemaphoreType.DMA((2, 2, n_buf))],
        ),
        out_shape=jax.ShapeDtypeStruct((n_seq * N_HEADS, V7X_SUBLANES, hd), F32), name="moba_decode_attn",
        compiler_params=pltpu.CompilerParams(dimension_semantics=("arbitrary",)),
    )(sel, page_table, rel_bias, q_rows, k_new_rows, v_new_rows, cache_k, cache_v)


ROW_TILE = 512
COL_TILE = 1024
DEC_ROWS = 16
CONV_TIME_TILE = 256


def kernel(x_prompt, x_sample, state_conv, cache_k, cache_v, page_table, norm_mix_g, norm_ffn_g, a_w_in, a_b_in, a_w_dw, a_b_dw, a_ln_g, a_ln_b, a_w_out, a_b_out, kv_norm_g, w_k, w_v, b_w_q, b_w_o, rel_bias, f_w_gate, f_w_up, f_w_down, final_norm_g):
    B, S, D = x_prompt.shape
    DB, dec_seq, _ = x_sample.shape
    assert dec_seq == 1, "decode path handles one new token per sequence"
    n_a = a_w_in.shape[0]
    depth = norm_mix_g.shape[0]
    F = f_w_gate.shape[-1]
    n_pool = cache_k.shape[0]
    hd = D // N_HEADS
    ppb = MOBA_BLOCK // PAGE_SIZE
    n_pages = page_table.shape[1]

    bf = lambda w: w.astype(BF16)
    a_w_in_b, a_w_out_b = bf(a_w_in), bf(a_w_out)
    w_k_b, w_v_b, w_q_b, w_o_b = bf(w_k), bf(w_v), bf(b_w_q), bf(b_w_o)
    w_gate_b, w_up_b, w_down_b = bf(f_w_gate), bf(f_w_up), bf(f_w_down)

    xp = x_prompt.reshape(B * S, D)
    xs = jnp.pad(x_sample.reshape(DB, D), ((0, DEC_ROWS - DB), (0, 0)))
    streams = [dict(x=xp, tm=ROW_TILE, tag="p"), dict(x=xs, tm=DEC_ROWS, tag="s")]
    mm = lambda st, *a, **kw: fused_matmul(*a, tm=st["tm"], **kw)

    conv_p, conv_s = [], []
    bias_tables = relbias_tables(rel_bias)
    k_rows = v_rows = None
    block_sums = None
    for l in range(depth):
        for st in streams:
            x = st["x"]
            if l < n_a:
                u = mm(st, x, [(a_w_in_b[l], 0), (a_w_in_b[l], D // COL_TILE)], D, gain=norm_mix_g[l],
                       biases=[(a_b_in[l].reshape(1, 2 * D), 0), (a_b_in[l].reshape(1, 2 * D), D // COL_TILE)],
                       epilogue="glu", tn=COL_TILE, name="glu_" + st["tag"])
                if st["tag"] == "p":
                    u3 = u.reshape(B, S, D)
                    left = jnp.zeros((B, CONV_WIDTH - 1, D), F32)
                    c = conv_ln_silu(u3, left, a_w_dw[l], a_b_dw[l], a_ln_g[l], a_ln_b[l],
                                     tt=CONV_TIME_TILE).reshape(B * S, D)
                    conv_p.append(u3[:, S - (CONV_WIDTH - 1):, :])
                else:
                    c = conv_step(state_conv[l], u[:DB], a_w_dw[l], a_b_dw[l], a_ln_g[l], a_ln_b[l])
                    c = jnp.pad(c, ((0, DEC_ROWS - DB), (0, 0)))
                    conv_s.append(jnp.concatenate([state_conv[l][:, 1:, :], u[:DB, None, :]], axis=1))
                x = mm(st, c, [(a_w_out_b[l], 0)], D, biases=[(a_b_out[l].reshape(1, D), 0)], residual=x,
                       tn=COL_TILE, name="convout_" + st["tag"])
            else:
                j = l - n_a
                if l == n_a:
                    kf, kb = mm(st, x, [(w_k_b, 0)], D, gain=kv_norm_g, out_dtypes=(F32, BF16), tn=COL_TILE,
                                name="kproj_" + st["tag"])
                    vf, vb = mm(st, x, [(w_v_b, 0)], D, gain=kv_norm_g, out_dtypes=(F32, BF16), tn=COL_TILE,
                                name="vproj_" + st["tag"])
                    st["kv"] = (kf, vf, kb, vb)
                kf, vf, kb, vb = st["kv"]
                if st["tag"] == "p":
                    q = mm(st, x, [(w_q_b[j], 0)], D, gain=norm_mix_g[l], out_dtypes=(BF16,),
                           out_scale=hd ** -0.5 * LOG2E, tn=COL_TILE, name="qproj_p")
                    o = moba_prompt_attn(q, kb, vb, bias_tables, n_seq=B, seq=S)
                else:
                    q = mm(st, x, [(w_q_b[j], 0)], D, gain=norm_mix_g[l], tn=COL_TILE, name="qproj_s")
                    if block_sums is None:
                        block_sums = page_block_sums(cache_k, page_table, ppb=ppb).reshape(DB, n_pages // ppb, D)
                    picks = moba_gate_topk(q, block_sums, kf, n_seq=DB, rows_per_block=MOBA_BLOCK)
                    sel = picks[:, :MOBA_TOPK, :N_HEADS].transpose(0, 2, 1)
                    o8 = moba_decode_attn(sel, page_table, rel_bias, q, kf, vf, cache_k, cache_v, n_seq=DB)
                    o = jnp.pad(o8[:, 0, :].reshape(DB, D), ((0, DEC_ROWS - DB), (0, 0))).astype(BF16)
                x = mm(st, o, [(w_o_b[j], 0)], D, residual=x, tn=COL_TILE, name="oproj_" + st["tag"])
            act = mm(st, x, [(w_gate_b[l], 0), (w_up_b[l], 0)], F, gain=norm_ffn_g[l], epilogue="swiglu",
                     out_dtypes=(BF16,), tn=F // 11, name="ffn_up_" + st["tag"])
            x = mm(st, act, [(w_down_b[l], 0)], D, residual=x, tn=COL_TILE // 2, name="ffn_down_" + st["tag"])
            st["x"] = x

    y_prompt = rmsnorm_rows(streams[0]["x"], final_norm_g, tm=ROW_TILE).reshape(B, S, D)
    y_sample = rmsnorm_rows(streams[1]["x"], final_norm_g, tm=DEC_ROWS)[:DB].reshape(DB, 1, D)
    kf_p, vf_p = streams[0]["kv"][:2]
    kf_s, vf_s = streams[1]["kv"][:2]
    return (y_prompt, y_sample, jnp.stack(conv_p), jnp.stack(conv_s),
            kf_p.reshape(B, S, N_HEADS, hd), vf_p.reshape(B, S, N_HEADS, hd),
            kf_s[:DB].reshape(DB, 1, N_HEADS, hd), vf_s[:DB].reshape(DB, 1, N_HEADS, hd))
```

```python
import functools
import math

import numpy as np
import jax
import jax.numpy as jnp
from jax import lax
from jax.experimental import pallas as pl
from jax.experimental.pallas import tpu as pltpu

F32 = jnp.float32
BF16 = jnp.bfloat16

N_HEADS = 16
CONV_WIDTH = 31
MOBA_BLOCK = 256
MOBA_TOPK = 3
NUM_BUCKETS = 32
MAX_DISTANCE = 128
PAGE_SIZE = 128
EPS = 1e-6
NEG = -1e30
LOG2E = math.log2(math.e)

V7X_VMEM_BYTES = 64 * 1024 * 1024
V7X_LANES = 128
V7X_SUBLANES = 8
VMEM_CAP = V7X_VMEM_BYTES - 4 * 1024 * 1024


def _vmem_limit(est_bytes):
    return int(min(max(est_bytes + (8 << 20), 16 << 20), VMEM_CAP))


def _bucket_thresholds():
    max_exact = NUM_BUCKETS // 2
    n = np.arange(0, 4 * MAX_DISTANCE, dtype=np.int64)
    nf = np.maximum(n, max_exact).astype(np.float32)
    large = max_exact + (np.log(nf / np.float32(max_exact)) / np.float32(math.log(MAX_DISTANCE / max_exact))
                         * np.float32(NUM_BUCKETS - max_exact)).astype(np.int32)
    large = np.minimum(large, NUM_BUCKETS - 1)
    bucket = np.where(n < max_exact, n, large)
    assert np.all(np.diff(bucket) >= 0) and bucket[-1] == NUM_BUCKETS - 1
    return [int(np.argmax(bucket >= k)) for k in range(1, NUM_BUCKETS)]


_BUCKET_START = _bucket_thresholds()


def _bias_from_distance(dist, table_at):
    bias = jnp.full(dist.shape, table_at(0), F32)
    for k in range(1, NUM_BUCKETS):
        bias = jnp.where(dist >= _BUCKET_START[k - 1], table_at(k), bias)
    return bias


def _mm_body(*refs, has_norm, n_w, has_bias, has_res, epilogue, out_scale, n_out):
    it = iter(refs)
    x_ref = next(it)
    g_ref = next(it) if has_norm else None
    w_refs = [next(it) for _ in range(n_w)]
    b_refs = [next(it) for _ in range(n_w)] if has_bias else []
    res_ref = next(it) if has_res else None
    out_refs = [next(it) for _ in range(n_out)]
    if has_norm:
        xn_ref = next(it)

        @pl.when(pl.program_id(1) == 0)
        def _():
            x = x_ref[...]
            ms = jnp.mean(x * x, axis=-1, keepdims=True)
            xn_ref[...] = (x * lax.rsqrt(ms + EPS) * g_ref[...]).astype(xn_ref.dtype)

        xb = xn_ref[...]
    else:
        xb = x_ref[...]
    accs = [jnp.dot(xb, w[...], preferred_element_type=F32) for w in w_refs]
    if has_bias:
        accs = [a + b[...] for a, b in zip(accs, b_refs)]
    if epilogue == "glu":
        y = accs[0] * jax.nn.sigmoid(accs[1])
    elif epilogue == "swiglu":
        y = jax.nn.silu(accs[0]) * accs[1]
    else:
        y = accs[0]
    if out_scale is not None:
        y = y * out_scale
    if has_res:
        y = res_ref[...] + y
    for o in out_refs:
        if len(o.shape) == 3:
            hd = o.shape[-1]
            for h in range(o.shape[1]):
                o[:, h, :] = y[:, h * hd:(h + 1) * hd].astype(o.dtype)
        else:
            o[...] = y.astype(o.dtype)


MATMUL_VMEM_BUDGET = 48 * 1024 * 1024


def fused_matmul(x, ws, n_cols, *, gain=None, biases=None, residual=None, epilogue="none",
                 outs=((F32, None),), out_scale=None, tm, name):
    M, K = x.shape
    has_norm = gain is not None
    out_bytes = sum(jnp.dtype(dt).itemsize for dt, _ in outs) + (4 if residual is not None else 0)

    def estimate(tn):
        return (2 * tm * K * x.dtype.itemsize + (tm * K * 2 if has_norm else 0) + 2 * len(ws) * K * tn * 2
                + 2 * tm * tn * out_bytes + (len(ws) + 1) * tm * tn * 4)

    def legal(tn):
        heads_ok = all(hd is None or (tn // hd) % V7X_SUBLANES == 0 or tn == n_cols for _, hd in outs)
        offs_ok = all(off % tn == 0 for _, _, off in ws)
        return n_cols % tn == 0 and heads_ok and offs_ok

    tiles = [t for t in (2048, 1024, 512, 256, 128) if legal(t)]
    fitting = [t for t in tiles if estimate(t) <= MATMUL_VMEM_BUDGET]
    tn = fitting[0] if fitting else tiles[-1]
    assert M % tm == 0
    grid = (M // tm, n_cols // tn)
    in_specs = [pl.BlockSpec((tm, K), lambda i, j: (i, 0))]
    args = [x]
    if has_norm:
        in_specs.append(pl.BlockSpec((1, K), lambda i, j: (0, 0)))
        args.append(gain.reshape(1, K))
    for w, layer, off in ws:
        in_specs.append(pl.BlockSpec((None, K, tn), lambda i, j, layer=layer, ob=off // tn: (layer, 0, j + ob)))
        args.append(w)
    if biases is not None:
        for b, off in biases:
            in_specs.append(pl.BlockSpec((1, tn), lambda i, j, ob=off // tn: (0, j + ob)))
            args.append(b)
    if residual is not None:
        in_specs.append(pl.BlockSpec((tm, tn), lambda i, j: (i, j)))
        args.append(residual)
    out_specs, out_shape = [], []
    for dt, hd in outs:
        if hd is None:
            out_specs.append(pl.BlockSpec((tm, tn), lambda i, j: (i, j)))
            out_shape.append(jax.ShapeDtypeStruct((M, n_cols), dt))
        else:
            out_specs.append(pl.BlockSpec((tm, tn // hd, hd), lambda i, j: (i, j, 0)))
            out_shape.append(jax.ShapeDtypeStruct((M, n_cols // hd, hd), dt))
    scratch = [pltpu.VMEM((tm, K), BF16)] if has_norm else []
    body = functools.partial(_mm_body, has_norm=has_norm, n_w=len(ws), has_bias=biases is not None,
                             has_res=residual is not None, epilogue=epilogue, out_scale=out_scale,
                             n_out=len(outs))
    res = pl.pallas_call(
        body, grid=grid, in_specs=in_specs, out_specs=out_specs, out_shape=out_shape,
        scratch_shapes=scratch, name=name,
        compiler_params=pltpu.CompilerParams(dimension_semantics=("arbitrary", "arbitrary"),
                                             vmem_limit_bytes=_vmem_limit(estimate(tn))),
    )(*args)
    return res if len(res) > 1 else res[0]


def _rmsnorm_body(x_ref, g_ref, o_ref):
    x = x_ref[...]
    ms = jnp.mean(x * x, axis=-1, keepdims=True)
    o_ref[...] = x * lax.rsqrt(ms + EPS) * g_ref[...]


def rmsnorm_rows(x, gain, *, tm):
    M, D = x.shape
    return pl.pallas_call(
        _rmsnorm_body, grid=(M // tm,),
        in_specs=[pl.BlockSpec((tm, D), lambda i: (i, 0)), pl.BlockSpec((1, D), lambda i: (0, 0))],
        out_specs=pl.BlockSpec((tm, D), lambda i: (i, 0)),
        out_shape=jax.ShapeDtypeStruct((M, D), F32), name="rmsnorm_rows",
        compiler_params=pltpu.CompilerParams(dimension_semantics=("arbitrary",),
                                             vmem_limit_bytes=_vmem_limit(4 * tm * D * 4)),
    )(x, gain.reshape(1, D))


CONV_HALO = 32
CONV_ROWS = 64


def _ln_silu(c, g, b):
    mu = jnp.mean(c, axis=-1, keepdims=True)
    xc = c - mu
    var = jnp.mean(xc * xc, axis=-1, keepdims=True)
    y = xc * lax.rsqrt(var + EPS) * g + b
    return y * jax.nn.sigmoid(y)


def _conv_body(u_ref, halo_ref, left_ref, w_ref, bdw_ref, g_ref, b_ref, o_ref, ext_ref, y_ref, *, tt):
    first = pl.program_id(1) == 0
    ext_ref[0:CONV_HALO, :] = jnp.where(first, left_ref[...], halo_ref[...])
    ext_ref[CONV_HALO:, :] = u_ref[...]
    d_model = u_ref.shape[-1]
    lead = CONV_HALO - (CONV_WIDTH - 1)

    def chunk(c, carry):
        lanes = pl.ds(pl.multiple_of(c * V7X_LANES, V7X_LANES), V7X_LANES)
        w = w_ref[:, lanes]
        for r0 in range(0, tt, CONV_ROWS):
            n_win = CONV_ROWS + CONV_HALO
            win = ext_ref[r0:r0 + n_win, lanes]
            acc = jnp.zeros((CONV_ROWS, V7X_LANES), F32)
            for res in range(V7X_SUBLANES):
                shifted = win if res == 0 else pltpu.roll(win, n_win - res, axis=0)
                for off in range(res, lead + CONV_WIDTH, V7X_SUBLANES):
                    k = off - lead
                    if k >= 0:
                        acc = acc + shifted[off - res:off - res + CONV_ROWS, :] * w[k:k + 1, :]
            y_ref[r0:r0 + CONV_ROWS, lanes] = acc
        return carry

    lax.fori_loop(0, d_model // V7X_LANES, chunk, 0)
    c = y_ref[...] + bdw_ref[...]
    o_ref[...] = _ln_silu(c, g_ref[...], b_ref[...]).astype(o_ref.dtype)


def conv_ln_silu(u, left, w_dw, b_dw, ln_g, ln_b, *, tt):
    B, T, D = u.shape
    left_pad = jnp.pad(left, ((0, 0), (CONV_HALO - (CONV_WIDTH - 1), 0), (0, 0)))
    hb = tt // CONV_HALO
    row = lambda a: a.reshape(1, D)
    est = 2 * tt * D * 4 + 4 * CONV_HALO * D * 4 + (2 * tt + CONV_HALO) * D * 4 + 2 * tt * D * 2 + 6 * tt * D * 4
    return pl.pallas_call(
        functools.partial(_conv_body, tt=tt), grid=(B, T // tt),
        in_specs=[
            pl.BlockSpec((None, tt, D), lambda b, i: (b, i, 0)),
            pl.BlockSpec((None, CONV_HALO, D), lambda b, i: (b, jnp.maximum(i * hb - 1, 0), 0)),
            pl.BlockSpec((None, CONV_HALO, D), lambda b, i: (b, 0, 0)),
            pl.BlockSpec((CONV_WIDTH, D), lambda b, i: (0, 0)),
            pl.BlockSpec((1, D), lambda b, i: (0, 0)),
            pl.BlockSpec((1, D), lambda b, i: (0, 0)),
            pl.BlockSpec((1, D), lambda b, i: (0, 0)),
        ],
        out_specs=pl.BlockSpec((None, tt, D), lambda b, i: (b, i, 0)),
        out_shape=jax.ShapeDtypeStruct((B, T, D), BF16),
        scratch_shapes=[pltpu.VMEM((tt + CONV_HALO, D), F32), pltpu.VMEM((tt, D), F32)],
        name="conv_ln_silu",
        compiler_params=pltpu.CompilerParams(dimension_semantics=("arbitrary", "arbitrary"),
                                             vmem_limit_bytes=_vmem_limit(est)),
    )(u, u, left_pad, w_dw, row(b_dw), row(ln_g), row(ln_b))


def _conv_step_body(state_ref, u_ref, w_ref, bdw_ref, g_ref, b_ref, o_ref):
    acc = u_ref[...] * w_ref[CONV_WIDTH - 1:CONV_WIDTH, :]
    for k in range(CONV_WIDTH - 1):
        acc = acc + state_ref[:, k, :] * w_ref[k:k + 1, :]
    c = acc + bdw_ref[...]
    o_ref[...] = _ln_silu(c, g_ref[...], b_ref[...]).astype(o_ref.dtype)


def conv_step(state, u_new, w_dw, b_dw, ln_g, ln_b):
    NB, _, D = state.shape
    row = lambda a: a.reshape(1, D)
    return pl.pallas_call(
        _conv_step_body,
        out_shape=jax.ShapeDtypeStruct((NB, D), BF16), name="conv_step",
    )(state, u_new, w_dw, row(b_dw), row(ln_g), row(ln_b))


def _relbias_body(table_ref, o_ref):
    h = pl.program_id(0)
    a = lax.broadcasted_iota(jnp.int32, (MOBA_BLOCK, MOBA_BLOCK), 0)
    b = lax.broadcasted_iota(jnp.int32, (MOBA_BLOCK, MOBA_BLOCK), 1)
    far = table_ref[NUM_BUCKETS - 1, h]
    table_at = lambda k: table_ref[k, h]
    d_own = a - b
    own = (_bias_from_distance(d_own, table_at) - far) * LOG2E
    o_ref[0] = jnp.where(d_own >= 0, own, NEG)
    o_ref[1] = (_bias_from_distance(d_own + MOBA_BLOCK, table_at) - far) * LOG2E


def relbias_tables(rel_bias):
    return pl.pallas_call(
        _relbias_body, grid=(N_HEADS,),
        in_specs=[pl.BlockSpec(memory_space=pltpu.SMEM)],
        out_specs=pl.BlockSpec((None, 2, MOBA_BLOCK, MOBA_BLOCK), lambda h: (h, 0, 0, 0)),
        out_shape=jax.ShapeDtypeStruct((N_HEADS, 2, MOBA_BLOCK, MOBA_BLOCK), F32), name="relbias_tables",
    )(rel_bias)


GATE_ROWS = 16


def _moba_prompt_body(q_ref, k_ref, v_ref, bias_ref, o_ref, kx_ref, *, nb):
    blk = MOBA_BLOCK
    hd = q_ref.shape[-1]
    seq = nb * blk
    nt = (((1,), (1,)), ((), ()))

    k = k_ref[...]
    kx_ref[:, 0:hd] = k
    kr = lax.broadcasted_iota(jnp.int32, (seq, V7X_LANES), 0)
    kc = lax.broadcasted_iota(jnp.int32, (seq, V7X_LANES), 1)
    kx_ref[:, hd:] = (kr // blk == kc).astype(BF16)
    r = lax.broadcasted_iota(jnp.int32, (GATE_ROWS, seq), 0)
    c = lax.broadcasted_iota(jnp.int32, (GATE_ROWS, seq), 1)
    avg = jnp.where(c // blk == r, 1.0 / blk, 0.0).astype(BF16)
    km = jnp.dot(avg, k, preferred_element_type=F32)
    km_hi = km.astype(BF16)
    km_lo = (km - km_hi.astype(F32)).astype(BF16)

    q = q_ref[...]
    g2 = (lax.dot_general(km_hi, q, nt, preferred_element_type=F32)
          + lax.dot_general(km_lo, q, nt, preferred_element_type=F32))
    gate = g2[0:V7X_SUBLANES]
    row = lax.broadcasted_iota(jnp.int32, gate.shape, 0)
    own_blk = lax.broadcasted_iota(jnp.int32, gate.shape, 1) // blk
    gate = jnp.where(row < own_blk, gate, NEG)
    beaten = jnp.zeros(gate.shape, jnp.int32)
    for jp in range(nb):
        gj = gate[jp:jp + 1, :]
        beaten = beaten + ((gj > gate) | ((gj == gate) & (jp < row))).astype(jnp.int32)
    chosen = (beaten < MOBA_TOPK) & (row < own_blk)
    mask_t = jnp.where(chosen | (row == own_blk), 0.0, NEG)
    mask_t = jnp.concatenate([mask_t, jnp.zeros((V7X_LANES - V7X_SUBLANES, seq), F32)], axis=0)
    qx = jnp.concatenate([q, mask_t.T.astype(BF16)], axis=1)

    for own in range(nb):
        n = (own + 1) * blk
        s = lax.dot_general(qx[own * blk:n, :], kx_ref[0:n, :], nt, preferred_element_type=F32)
        parts = [s[:, n - blk:] + bias_ref[0]]
        if own >= 1:
            parts.insert(0, s[:, n - 2 * blk:n - blk] + bias_ref[1])
        if own >= 2:
            parts.insert(0, s[:, :n - 2 * blk])
        s = jnp.concatenate(parts, axis=1) if len(parts) > 1 else parts[0]
        m = jnp.max(s, axis=-1, keepdims=True)
        p = jnp.exp2(s - m)
        l = jnp.sum(p, axis=-1, keepdims=True)
        pv = jnp.dot(p.astype(BF16), v_ref[0:n, :], preferred_element_type=F32)
        o_ref[own * blk:n, :] = (pv * (1.0 / l)).astype(o_ref.dtype)


def moba_prompt_attn(q, k, v, bias, *, n_seq, seq):
    M, HD = q.shape
    hd = HD // N_HEADS
    nb = seq // MOBA_BLOCK
    assert seq % MOBA_BLOCK == 0 and nb <= V7X_SUBLANES
    head_spec = pl.BlockSpec((seq, hd), lambda b, h: (b, h))
    return pl.pallas_call(
        functools.partial(_moba_prompt_body, nb=nb), grid=(n_seq, N_HEADS),
        in_specs=[head_spec, head_spec, head_spec,
                  pl.BlockSpec((None, 2, MOBA_BLOCK, MOBA_BLOCK), lambda b, h: (h, 0, 0, 0))],
        out_specs=head_spec,
        out_shape=jax.ShapeDtypeStruct((M, HD), BF16),
        scratch_shapes=[pltpu.VMEM((seq, hd + V7X_LANES), BF16)],
        name="moba_prompt_attn",
        compiler_params=pltpu.CompilerParams(dimension_semantics=("arbitrary", "arbitrary"),
                                             vmem_limit_bytes=_vmem_limit(8 * MOBA_BLOCK * seq * 4)),
    )(q, k, v, bias)


PAGES_PER_STEP = 8


def _page_sums_body(pt_ref, *refs, ppb):
    page_refs, o_ref = refs[:-1], refs[-1]
    for j in range(len(page_refs) // ppb):
        tot = jnp.sum(page_refs[j * ppb][...], axis=0)
        for p in range(1, ppb):
            tot = tot + jnp.sum(page_refs[j * ppb + p][...], axis=0)
        o_ref[j] = tot


def page_block_sums(cache_k, page_table, *, ppb):
    _, page, H, hd = cache_k.shape
    DB, n_pages = page_table.shape
    nblk = n_pages // ppb
    pps = PAGES_PER_STEP
    assert pps % ppb == 0 and n_pages % pps == 0

    def page_spec(k):
        return pl.BlockSpec((None, page, H, hd), lambda b, j, pt: (pt[b, j * pps + k], 0, 0, 0))

    return pl.pallas_call(
        functools.partial(_page_sums_body, ppb=ppb),
        grid_spec=pltpu.PrefetchScalarGridSpec(
            num_scalar_prefetch=1, grid=(DB, n_pages // pps),
            in_specs=[page_spec(k) for k in range(pps)],
            out_specs=pl.BlockSpec((None, pps // ppb, H, hd), lambda b, j, pt: (b, j, 0, 0)),
        ),
        out_shape=jax.ShapeDtypeStruct((DB, nblk, H, hd), F32), name="page_block_sums",
        compiler_params=pltpu.CompilerParams(
            dimension_semantics=("arbitrary", "arbitrary"),
            vmem_limit_bytes=_vmem_limit(2 * pps * page * H * hd * 4)),
    )(page_table, *([cache_k] * pps))


def _gate_topk_body(q_ref, sums_ref, knew_ref, o_ref, *, nbp, own, inv_rows):
    b = pl.program_id(0)
    HD = q_ref.shape[-1]
    hd = HD // N_HEADS
    q = q_ref[pl.ds(b, 1), :]
    r = lax.broadcasted_iota(jnp.int32, (HD, V7X_LANES), 0)
    c = lax.broadcasted_iota(jnp.int32, (HD, V7X_LANES), 1)
    ind = (r // hd == c).astype(BF16)

    def head_sums(prod):
        hi = prod.astype(BF16)
        mid = (prod - hi.astype(F32)).astype(BF16)
        lo = (prod - hi.astype(F32) - mid.astype(F32)).astype(BF16)
        return (jnp.dot(hi, ind, preferred_element_type=F32) + jnp.dot(mid, ind, preferred_element_type=F32)
                + jnp.dot(lo, ind, preferred_element_type=F32))

    means = sums_ref[...] * inv_rows
    gate_past = head_sums(means * q)
    k_new = jnp.broadcast_to(knew_ref[pl.ds(b, 1), :], (V7X_SUBLANES, HD))
    gate_rec = head_sums((k_new * inv_rows) * q)
    gate = jnp.concatenate([gate_past, gate_rec], axis=0)
    idx = lax.broadcasted_iota(jnp.int32, gate.shape, 0)
    gate = jnp.where(idx < own, gate, NEG)
    gate = jnp.where(idx <= nbp, gate, -jnp.inf)
    picks = []
    for _ in range(MOBA_TOPK):
        mx = jnp.max(gate, axis=0, keepdims=True)
        pick = jnp.min(jnp.where(gate == mx, idx, nbp + V7X_SUBLANES), axis=0, keepdims=True)
        picks.append(pick)
        gate = jnp.where(idx == pick, -jnp.inf, gate)
    picks.append(jnp.zeros((V7X_SUBLANES - MOBA_TOPK, V7X_LANES), jnp.int32))
    o_ref[...] = jnp.concatenate(picks, axis=0)


def moba_gate_topk(q_rows, block_sums, k_new_rows, *, n_seq, rows_per_block):
    R, HD = q_rows.shape
    _, nbp, _ = block_sums.shape
    return pl.pallas_call(
        functools.partial(_gate_topk_body, nbp=nbp, own=nbp, inv_rows=1.0 / rows_per_block), grid=(n_seq,),
        in_specs=[
            pl.BlockSpec((R, HD), lambda b: (0, 0)),
            pl.BlockSpec((None, nbp, HD), lambda b: (b, 0, 0)),
            pl.BlockSpec((R, HD), lambda b: (0, 0)),
        ],
        out_specs=pl.BlockSpec((None, V7X_SUBLANES, V7X_LANES), lambda b: (b, 0, 0)),
        out_shape=jax.ShapeDtypeStruct((n_seq, V7X_SUBLANES, V7X_LANES), jnp.int32), name="moba_gate_topk",
        compiler_params=pltpu.CompilerParams(dimension_semantics=("arbitrary",)),
    )(q_rows, block_sums, k_new_rows)


def _decode_attn_body(sel_ref, pt_ref, table_ref, q_ref, kn_ref, vn_ref, ck_ref, cv_ref, o_ref,
                      kbuf, vbuf, sem, *, nbp, ppb, t_new, scale):
    step = pl.program_id(0)
    hd = kbuf.shape[-1]

    def page_copies(st, slot):
        b, h = st // N_HEADS, st % N_HEADS
        out = []
        for kk in range(MOBA_TOPK):
            blk = jnp.minimum(sel_ref[b, h, kk], nbp - 1)
            for p in range(ppb):
                page = pt_ref[b, blk * ppb + p]
                i = kk * ppb + p
                out.append(pltpu.make_async_copy(ck_ref.at[page, :, h, :], kbuf.at[slot, i], sem.at[0, slot, i]))
                out.append(pltpu.make_async_copy(cv_ref.at[page, :, h, :], vbuf.at[slot, i], sem.at[1, slot, i]))
        return out

    slot = step % 2

    @pl.when(step == 0)
    def _():
        for c in page_copies(step, slot):
            c.start()

    @pl.when(step + 1 < pl.num_programs(0))
    def _():
        for c in page_copies(step + 1, 1 - slot):
            c.start()

    for c in page_copies(step, slot):
        c.wait()

    b, h = step // N_HEADS, step % N_HEADS
    lanes = pl.ds(pl.multiple_of(h * hd, hd), hd)

    def rows8(ref):
        x = ref[:, lanes]
        r = lax.broadcasted_iota(jnp.int32, x.shape, 0)
        row = jnp.sum(jnp.where(r == b, x, 0.0), axis=0, keepdims=True)
        return jnp.broadcast_to(row, (V7X_SUBLANES, hd))

    q = rows8(q_ref)
    qb = q.astype(BF16)
    nt = (((1,), (1,)), ((), ()))
    table_at = lambda k: table_ref[k, h]
    kn, vn = rows8(kn_ref), rows8(vn_ref)
    l_new = jnp.sum(q * kn, axis=-1, keepdims=True) * scale + table_at(0)
    logits, values = [], []
    for kk in range(MOBA_TOPK):
        sel = sel_ref[b, h, kk]
        valid = sel < nbp
        selp = jnp.minimum(sel, nbp - 1)
        for p in range(ppb):
            kp = kbuf[slot, kk * ppb + p].astype(BF16)
            s = lax.dot_general(qb, kp, nt, preferred_element_type=F32) * scale
            pos = selp * MOBA_BLOCK + p * PAGE_SIZE + lax.broadcasted_iota(jnp.int32, s.shape, 1)
            s = s + _bias_from_distance(t_new - pos, table_at)
            logits.append(jnp.where(valid, s, NEG))
            values.append(vbuf[slot, kk * ppb + p].astype(BF16))
    m = l_new
    for s in logits:
        m = jnp.maximum(m, jnp.max(s, axis=-1, keepdims=True))
    p_new = jnp.exp(l_new - m)
    den = p_new
    acc = p_new * vn
    for s, v in zip(logits, values):
        p = jnp.exp(s - m)
        den = den + jnp.sum(p, axis=-1, keepdims=True)
        acc = acc + jnp.dot(p.astype(BF16), v, preferred_element_type=F32)
    o_ref[...] = acc / den


def moba_decode_attn(sel, page_table, rel_bias, q_rows, k_new_rows, v_new_rows, cache_k, cache_v, *, n_seq):
    R, HD = q_rows.shape
    hd = HD // N_HEADS
    n_pages = page_table.shape[1]
    ppb = MOBA_BLOCK // PAGE_SIZE
    nbp = n_pages // ppb
    assert n_pages % ppb == 0, "past length must be a whole number of MoBA blocks"
    row_spec = pl.BlockSpec((R, HD), lambda s, sel, pt: (0, 0))
    n_buf = MOBA_TOPK * ppb
    body = functools.partial(_decode_attn_body, nbp=nbp, ppb=ppb, t_new=n_pages * PAGE_SIZE, scale=hd ** -0.5)
    return pl.pallas_call(
        body,
        grid_spec=pltpu.PrefetchScalarGridSpec(
            num_scalar_prefetch=2, grid=(n_seq * N_HEADS,),
            in_specs=[pl.BlockSpec(memory_space=pltpu.SMEM), row_spec, row_spec, row_spec,
                      pl.BlockSpec(memory_space=pl.ANY), pl.BlockSpec(memory_space=pl.ANY)],
            out_specs=pl.BlockSpec((None, V7X_SUBLANES, hd), lambda s, sel, pt: (s, 0, 0)),
            scratch_shapes=[pltpu.VMEM((2, n_buf, PAGE_SIZE, hd), F32), pltpu.VMEM((2, n_buf, PAGE_SIZE, hd), F32),
                            pltpu.SemaphoreType.DMA((2, 2, n_buf))],
        ),
        out_shape=jax.ShapeDtypeStruct((n_seq * N_HEADS, V7X_SUBLANES, hd), F32), name="moba_decode_attn",
        compiler_params=pltpu.CompilerParams(dimension_semantics=("arbitrary",)),
    )(sel, page_table, rel_bias, q_rows, k_new_rows, v_new_rows, cache_k, cache_v)


ROW_TILE = 1024
DEC_ROWS = 16
CONV_TIME_TILE = 256


def kernel(x_prompt, x_sample, state_conv, cache_k, cache_v, page_table, norm_mix_g, norm_ffn_g, a_w_in, a_b_in, a_w_dw, a_b_dw, a_ln_g, a_ln_b, a_w_out, a_b_out, kv_norm_g, w_k, w_v, b_w_q, b_w_o, rel_bias, f_w_gate, f_w_up, f_w_down, final_norm_g):
    B, S, D = x_prompt.shape
    DB, dec_seq, _ = x_sample.shape
    assert dec_seq == 1, "decode path handles one new token per sequence"
    n_a = a_w_in.shape[0]
    depth = norm_mix_g.shape[0]
    F = f_w_gate.shape[-1]
    hd = D // N_HEADS
    ppb = MOBA_BLOCK // PAGE_SIZE
    n_pages = page_table.shape[1]

    bf = lambda w: w.astype(BF16)
    a_w_in_b, a_w_out_b = bf(a_w_in), bf(a_w_out)
    w_k_b, w_v_b, w_q_b, w_o_b = bf(w_k)[None], bf(w_v)[None], bf(b_w_q), bf(b_w_o)
    w_gate_b, w_up_b, w_down_b = bf(f_w_gate), bf(f_w_up), bf(f_w_down)

    xp = x_prompt.reshape(B * S, D)
    xs = jnp.pad(x_sample.reshape(DB, D), ((0, DEC_ROWS - DB), (0, 0)))
    streams = [dict(x=xp, tm=ROW_TILE, tag="p"), dict(x=xs, tm=DEC_ROWS, tag="s")]
    mm = lambda st, *a, **kw: fused_matmul(*a, tm=st["tm"], **kw)

    conv_p, conv_s = [], []
    bias_tables = relbias_tables(rel_bias)
    block_sums = None
    for l in range(depth):
        for st in streams:
            x = st["x"]
            if l < n_a:
                b_in = a_b_in[l].reshape(1, 2 * D)
                u = mm(st, x, [(a_w_in_b, l, 0), (a_w_in_b, l, D)], D, gain=norm_mix_g[l],
                       biases=[(b_in, 0), (b_in, D)], epilogue="glu", name="glu_" + st["tag"])
                if st["tag"] == "p":
                    u3 = u.reshape(B, S, D)
                    left = jnp.zeros((B, CONV_WIDTH - 1, D), F32)
                    c = conv_ln_silu(u3, left, a_w_dw[l], a_b_dw[l], a_ln_g[l], a_ln_b[l],
                                     tt=CONV_TIME_TILE).reshape(B * S, D)
                    conv_p.append(u3[:, S - (CONV_WIDTH - 1):, :])
                else:
                    c = conv_step(state_conv[l], u[:DB], a_w_dw[l], a_b_dw[l], a_ln_g[l], a_ln_b[l])
                    c = jnp.pad(c, ((0, DEC_ROWS - DB), (0, 0)))
                    conv_s.append(jnp.concatenate([state_conv[l][:, 1:, :], u[:DB, None, :]], axis=1))
                x = mm(st, c, [(a_w_out_b, l, 0)], D, biases=[(a_b_out[l].reshape(1, D), 0)], residual=x,
                       name="convout_" + st["tag"])
            else:
                j = l - n_a
                if l == n_a:
                    kv_outs = ((F32, hd), (BF16, None)) if st["tag"] == "p" else ((F32, None),)
                    st["k"] = mm(st, x, [(w_k_b, 0, 0)], D, gain=kv_norm_g, outs=kv_outs, name="kproj_" + st["tag"])
                    st["v"] = mm(st, x, [(w_v_b, 0, 0)], D, gain=kv_norm_g, outs=kv_outs, name="vproj_" + st["tag"])
                if st["tag"] == "p":
                    q = mm(st, x, [(w_q_b, j, 0)], D, gain=norm_mix_g[l], outs=((BF16, None),),
                           out_scale=hd ** -0.5 * LOG2E, name="qproj_p")
                    o = moba_prompt_attn(q, st["k"][1], st["v"][1], bias_tables, n_seq=B, seq=S)
                else:
                    q = mm(st, x, [(w_q_b, j, 0)], D, gain=norm_mix_g[l], name="qproj_s")
                    if block_sums is None:
                        block_sums = page_block_sums(cache_k, page_table, ppb=ppb).reshape(DB, n_pages // ppb, D)
                    picks = moba_gate_topk(q, block_sums, st["k"], n_seq=DB, rows_per_block=MOBA_BLOCK)
                    sel = picks[:, :MOBA_TOPK, :N_HEADS].transpose(0, 2, 1)
                    o8 = moba_decode_attn(sel, page_table, rel_bias, q, st["k"], st["v"], cache_k, cache_v,
                                          n_seq=DB)
                    o = jnp.pad(o8[:, 0, :].reshape(DB, D), ((0, DEC_ROWS - DB), (0, 0))).astype(BF16)
                x = mm(st, o, [(w_o_b, j, 0)], D, residual=x, name="oproj_" + st["tag"])
            act = mm(st, x, [(w_gate_b, l, 0), (w_up_b, l, 0)], F, gain=norm_ffn_g[l], epilogue="swiglu",
                     outs=((BF16, None),), name="ffn_up_" + st["tag"])
            x = mm(st, act, [(w_down_b, l, 0)], D, residual=x, name="ffn_down_" + st["tag"])
            st["x"] = x

    y_prompt = rmsnorm_rows(streams[0]["x"], final_norm_g, tm=ROW_TILE // 2).reshape(B, S, D)
    y_sample = rmsnorm_rows(streams[1]["x"], final_norm_g, tm=DEC_ROWS)[:DB].reshape(DB, 1, D)
    k_s, v_s = streams[1]["k"], streams[1]["v"]
    return (y_prompt, y_sample, jnp.stack(conv_p), jnp.stack(conv_s),
            streams[0]["k"][0].reshape(B, S, N_HEADS, hd), streams[0]["v"][0].reshape(B, S, N_HEADS, hd),
            k_s[:DB].reshape(DB, 1, N_HEADS, hd), v_s[:DB].reshape(DB, 1, N_HEADS, hd))
```

```python
import functools
import math

import numpy as np
import jax
import jax.numpy as jnp
from jax import lax
from jax.experimental import pallas as pl
from jax.experimental.pallas import tpu as pltpu

F32 = jnp.float32
BF16 = jnp.bfloat16

N_HEADS = 16
CONV_WIDTH = 31
MOBA_BLOCK = 256
MOBA_TOPK = 3
NUM_BUCKETS = 32
MAX_DISTANCE = 128
PAGE_SIZE = 128
EPS = 1e-6
NEG = -1e30
LOG2E = math.log2(math.e)

V7X_VMEM_BYTES = 64 * 1024 * 1024
V7X_LANES = 128
V7X_SUBLANES = 8
VMEM_CAP = V7X_VMEM_BYTES - 4 * 1024 * 1024


def _vmem_limit(est_bytes):
    return int(min(max(est_bytes + (8 << 20), 16 << 20), VMEM_CAP))


def _bucket_thresholds():
    max_exact = NUM_BUCKETS // 2
    n = np.arange(0, 4 * MAX_DISTANCE, dtype=np.int64)
    nf = np.maximum(n, max_exact).astype(np.float32)
    large = max_exact + (np.log(nf / np.float32(max_exact)) / np.float32(math.log(MAX_DISTANCE / max_exact))
                         * np.float32(NUM_BUCKETS - max_exact)).astype(np.int32)
    large = np.minimum(large, NUM_BUCKETS - 1)
    bucket = np.where(n < max_exact, n, large)
    assert np.all(np.diff(bucket) >= 0) and bucket[-1] == NUM_BUCKETS - 1
    return [int(np.argmax(bucket >= k)) for k in range(1, NUM_BUCKETS)]


_BUCKET_START = _bucket_thresholds()


def _bias_from_distance(dist, table_at):
    bias = jnp.full(dist.shape, table_at(0), F32)
    for k in range(1, NUM_BUCKETS):
        bias = jnp.where(dist >= _BUCKET_START[k - 1], table_at(k), bias)
    return bias


def _mm_body(*refs, has_norm, n_w, has_bias, has_res, epilogue, out_scale, n_out, n_pages):
    it = iter(refs)
    if n_pages:
        next(it)
    x_ref = next(it)
    g_ref = next(it) if has_norm else None
    w_refs = [next(it) for _ in range(n_w)]
    b_refs = [next(it) for _ in range(n_w)] if has_bias else []
    res_ref = next(it) if has_res else None
    page_refs = [next(it) for _ in range(n_pages)]
    out_refs = [next(it) for _ in range(n_out)]
    page_out = next(it) if n_pages else None
    if has_norm:
        xn_ref = next(it)

        @pl.when(pl.program_id(1) == 0)
        def _():
            x = x_ref[...]
            ms = jnp.mean(x * x, axis=-1, keepdims=True)
            xn_ref[...] = (x * lax.rsqrt(ms + EPS) * g_ref[...]).astype(xn_ref.dtype)

        xb = xn_ref[...]
    else:
        xb = x_ref[...]
    accs = [jnp.dot(xb, w[...], preferred_element_type=F32) for w in w_refs]
    if has_bias:
        accs = [a + b[...] for a, b in zip(accs, b_refs)]
    if epilogue == "glu":
        y = accs[0] * jax.nn.sigmoid(accs[1])
    elif epilogue == "swiglu":
        y = jax.nn.silu(accs[0]) * accs[1]
    else:
        y = accs[0]
    if out_scale is not None:
        y = y * out_scale
    if has_res:
        y = res_ref[...] + y
    for o in out_refs:
        if len(o.shape) == 3:
            hd = o.shape[-1]
            for h in range(o.shape[1]):
                o[:, h, :] = y[:, h * hd:(h + 1) * hd].astype(o.dtype)
        else:
            o[...] = y.astype(o.dtype)
    for k, page_ref in enumerate(page_refs):
        page_out[k] = jnp.sum(page_ref[...], axis=0)


MATMUL_VMEM_BUDGET = 48 * 1024 * 1024


def fused_matmul(x, ws, n_cols, *, gain=None, biases=None, residual=None, epilogue="none",
                 outs=((F32, None),), out_scale=None, page_stream=None, tm, name):
    M, K = x.shape
    has_norm = gain is not None
    out_bytes = sum(jnp.dtype(dt).itemsize for dt, _ in outs) + (4 if residual is not None else 0)

    def pages_per_step(tn):
        return -(-page_stream[1].shape[0] // ((M // tm) * (n_cols // tn))) if page_stream is not None else 0

    def estimate(tn):
        page_bytes = 2 * pages_per_step(tn) * math.prod(page_stream[0].shape[1:]) * 4 if page_stream else 0
        return (2 * tm * K * x.dtype.itemsize + (tm * K * 2 if has_norm else 0) + 2 * len(ws) * K * tn * 2
                + 2 * tm * tn * out_bytes + (len(ws) + 1) * tm * tn * 4 + page_bytes)

    def legal(tn):
        heads_ok = all(hd is None or (tn // hd) % V7X_SUBLANES == 0 or tn == n_cols for _, hd in outs)
        offs_ok = all(off % tn == 0 for _, _, off in ws)
        return n_cols % tn == 0 and heads_ok and offs_ok

    tiles = [t for t in (2048, 1024, 512, 256, 128) if legal(t)]
    fitting = [t for t in tiles if estimate(t) <= MATMUL_VMEM_BUDGET]
    tn = fitting[0] if fitting else tiles[-1]
    assert M % tm == 0
    grid = (M // tm, n_cols // tn)
    in_specs = [pl.BlockSpec((tm, K), lambda i, j, *_:(i, 0))]
    args = [x]
    if has_norm:
        in_specs.append(pl.BlockSpec((1, K), lambda i, j, *_:(0, 0)))
        args.append(gain.reshape(1, K))
    for w, layer, off in ws:
        in_specs.append(pl.BlockSpec((None, K, tn), lambda i, j, *_, layer=layer, ob=off // tn: (layer, 0, j + ob)))
        args.append(w)
    if biases is not None:
        for b, off in biases:
            in_specs.append(pl.BlockSpec((1, tn), lambda i, j, *_, ob=off // tn: (0, j + ob)))
            args.append(b)
    if residual is not None:
        in_specs.append(pl.BlockSpec((tm, tn), lambda i, j, *_:(i, j)))
        args.append(residual)
    out_specs, out_shape = [], []
    for dt, hd in outs:
        if hd is None:
            out_specs.append(pl.BlockSpec((tm, tn), lambda i, j, *_:(i, j)))
            out_shape.append(jax.ShapeDtypeStruct((M, n_cols), dt))
        else:
            out_specs.append(pl.BlockSpec((tm, tn // hd, hd), lambda i, j, *_:(i, j, 0)))
            out_shape.append(jax.ShapeDtypeStruct((M, n_cols // hd, hd), dt))
    prefetch = []
    per_step = pages_per_step(tn)
    if page_stream is not None:
        cache, ids = page_stream
        nj = grid[1]
        n_slots = grid[0] * nj * per_step
        assert ids.shape[0] <= n_slots
        prefetch = [jnp.concatenate([ids, jnp.broadcast_to(ids[-1:], (n_slots - ids.shape[0],))])]
        for k in range(per_step):
            in_specs.append(pl.BlockSpec(
                (None,) + cache.shape[1:],
                lambda i, j, ids_ref, k=k: (ids_ref[(i * nj + j) * per_step + k], 0, 0, 0)))
            args.append(cache)
        out_specs.append(pl.BlockSpec((per_step,) + cache.shape[2:], lambda i, j, *_: (i * nj + j, 0, 0)))
        out_shape.append(jax.ShapeDtypeStruct((n_slots,) + cache.shape[2:], F32))
    scratch = [pltpu.VMEM((tm, K), BF16)] if has_norm else []
    body = functools.partial(_mm_body, has_norm=has_norm, n_w=len(ws), has_bias=biases is not None,
                             has_res=residual is not None, epilogue=epilogue, out_scale=out_scale,
                             n_out=len(outs), n_pages=per_step)
    res = pl.pallas_call(
        body,
        grid_spec=pltpu.PrefetchScalarGridSpec(num_scalar_prefetch=len(prefetch), grid=grid, in_specs=in_specs,
                                               out_specs=out_specs, scratch_shapes=scratch),
        out_shape=out_shape, name=name,
        compiler_params=pltpu.CompilerParams(dimension_semantics=("arbitrary", "arbitrary"),
                                             vmem_limit_bytes=_vmem_limit(estimate(tn))),
    )(*prefetch, *args)
    return res if len(res) > 1 else res[0]


def _rmsnorm_body(x_ref, g_ref, o_ref):
    x = x_ref[...]
    ms = jnp.mean(x * x, axis=-1, keepdims=True)
    o_ref[...] = x * lax.rsqrt(ms + EPS) * g_ref[...]


def rmsnorm_rows(x, gain, *, tm):
    M, D = x.shape
    return pl.pallas_call(
        _rmsnorm_body, grid=(M // tm,),
        in_specs=[pl.BlockSpec((tm, D), lambda i: (i, 0)), pl.BlockSpec((1, D), lambda i: (0, 0))],
        out_specs=pl.BlockSpec((tm, D), lambda i: (i, 0)),
        out_shape=jax.ShapeDtypeStruct((M, D), F32), name="rmsnorm_rows",
        compiler_params=pltpu.CompilerParams(dimension_semantics=("arbitrary",),
                                             vmem_limit_bytes=_vmem_limit(4 * tm * D * 4)),
    )(x, gain.reshape(1, D))


CONV_HALO = 32
CONV_ROWS = 64


def _ln_silu(c, g, b):
    mu = jnp.mean(c, axis=-1, keepdims=True)
    xc = c - mu
    var = jnp.mean(xc * xc, axis=-1, keepdims=True)
    y = xc * lax.rsqrt(var + EPS) * g + b
    return y * jax.nn.sigmoid(y)


def _conv_body(u_ref, halo_ref, left_ref, w_ref, bdw_ref, g_ref, b_ref, o_ref, ext_ref, y_ref, *, tt):
    first = pl.program_id(1) == 0
    ext_ref[0:CONV_HALO, :] = jnp.where(first, left_ref[...], halo_ref[...])
    ext_ref[CONV_HALO:, :] = u_ref[...]
    d_model = u_ref.shape[-1]
    lead = CONV_HALO - (CONV_WIDTH - 1)

    def chunk(c, carry):
        lanes = pl.ds(pl.multiple_of(c * V7X_LANES, V7X_LANES), V7X_LANES)
        w = w_ref[:, lanes]
        for r0 in range(0, tt, CONV_ROWS):
            n_win = CONV_ROWS + CONV_HALO
            win = ext_ref[r0:r0 + n_win, lanes]
            acc = jnp.zeros((CONV_ROWS, V7X_LANES), F32)
            for res in range(V7X_SUBLANES):
                shifted = win if res == 0 else pltpu.roll(win, n_win - res, axis=0)
                for off in range(res, lead + CONV_WIDTH, V7X_SUBLANES):
                    k = off - lead
                    if k >= 0:
                        acc = acc + shifted[off - res:off - res + CONV_ROWS, :] * w[k:k + 1, :]
            y_ref[r0:r0 + CONV_ROWS, lanes] = acc
        return carry

    lax.fori_loop(0, d_model // V7X_LANES, chunk, 0)
    c = y_ref[...] + bdw_ref[...]
    o_ref[...] = _ln_silu(c, g_ref[...], b_ref[...]).astype(o_ref.dtype)


def conv_ln_silu(u, left, w_dw, b_dw, ln_g, ln_b, *, tt):
    B, T, D = u.shape
    left_pad = jnp.pad(left, ((0, 0), (CONV_HALO - (CONV_WIDTH - 1), 0), (0, 0)))
    hb = tt // CONV_HALO
    row = lambda a: a.reshape(1, D)
    est = 2 * tt * D * 4 + 4 * CONV_HALO * D * 4 + (2 * tt + CONV_HALO) * D * 4 + 2 * tt * D * 2 + 6 * tt * D * 4
    return pl.pallas_call(
        functools.partial(_conv_body, tt=tt), grid=(B, T // tt),
        in_specs=[
            pl.BlockSpec((None, tt, D), lambda b, i: (b, i, 0)),
            pl.BlockSpec((None, CONV_HALO, D), lambda b, i: (b, jnp.maximum(i * hb - 1, 0), 0)),
            pl.BlockSpec((None, CONV_HALO, D), lambda b, i: (b, 0, 0)),
            pl.BlockSpec((CONV_WIDTH, D), lambda b, i: (0, 0)),
            pl.BlockSpec((1, D), lambda b, i: (0, 0)),
            pl.BlockSpec((1, D), lambda b, i: (0, 0)),
            pl.BlockSpec((1, D), lambda b, i: (0, 0)),
        ],
        out_specs=pl.BlockSpec((None, tt, D), lambda b, i: (b, i, 0)),
        out_shape=jax.ShapeDtypeStruct((B, T, D), BF16),
        scratch_shapes=[pltpu.VMEM((tt + CONV_HALO, D), F32), pltpu.VMEM((tt, D), F32)],
        name="conv_ln_silu",
        compiler_params=pltpu.CompilerParams(dimension_semantics=("arbitrary", "arbitrary"),
                                             vmem_limit_bytes=_vmem_limit(est)),
    )(u, u, left_pad, w_dw, row(b_dw), row(ln_g), row(ln_b))


def _conv_step_body(state_ref, u_ref, w_ref, bdw_ref, g_ref, b_ref, o_ref):
    acc = u_ref[...] * w_ref[CONV_WIDTH - 1:CONV_WIDTH, :]
    for k in range(CONV_WIDTH - 1):
        acc = acc + state_ref[:, k, :] * w_ref[k:k + 1, :]
    c = acc + bdw_ref[...]
    o_ref[...] = _ln_silu(c, g_ref[...], b_ref[...]).astype(o_ref.dtype)


def conv_step(state, u_new, w_dw, b_dw, ln_g, ln_b):
    NB, _, D = state.shape
    row = lambda a: a.reshape(1, D)
    return pl.pallas_call(
        _conv_step_body,
        out_shape=jax.ShapeDtypeStruct((NB, D), BF16), name="conv_step",
    )(state, u_new, w_dw, row(b_dw), row(ln_g), row(ln_b))


def _relbias_body(table_ref, o_ref):
    h = pl.program_id(0)
    a = lax.broadcasted_iota(jnp.int32, (MOBA_BLOCK, MOBA_BLOCK), 0)
    b = lax.broadcasted_iota(jnp.int32, (MOBA_BLOCK, MOBA_BLOCK), 1)
    far = table_ref[NUM_BUCKETS - 1, h]
    table_at = lambda k: table_ref[k, h]
    d_own = a - b
    own = (_bias_from_distance(d_own, table_at) - far) * LOG2E
    o_ref[0] = jnp.where(d_own >= 0, own, NEG)
    o_ref[1] = (_bias_from_distance(d_own + MOBA_BLOCK, table_at) - far) * LOG2E


def relbias_tables(rel_bias):
    return pl.pallas_call(
        _relbias_body, grid=(N_HEADS,),
        in_specs=[pl.BlockSpec(memory_space=pltpu.SMEM)],
        out_specs=pl.BlockSpec((None, 2, MOBA_BLOCK, MOBA_BLOCK), lambda h: (h, 0, 0, 0)),
        out_shape=jax.ShapeDtypeStruct((N_HEADS, 2, MOBA_BLOCK, MOBA_BLOCK), F32), name="relbias_tables",
    )(rel_bias)


GATE_ROWS = 16


def _moba_prompt_body(q_ref, k_ref, v_ref, bias_ref, o_ref, kx_ref, *, nb):
    blk = MOBA_BLOCK
    hd = q_ref.shape[-1]
    seq = nb * blk
    nt = (((1,), (1,)), ((), ()))

    k = k_ref[...]
    kx_ref[:, 0:hd] = k
    kr = lax.broadcasted_iota(jnp.int32, (seq, V7X_LANES), 0)
    kc = lax.broadcasted_iota(jnp.int32, (seq, V7X_LANES), 1)
    kx_ref[:, hd:] = (kr // blk == kc).astype(BF16)
    r = lax.broadcasted_iota(jnp.int32, (GATE_ROWS, seq), 0)
    c = lax.broadcasted_iota(jnp.int32, (GATE_ROWS, seq), 1)
    avg = jnp.where(c // blk == r, 1.0 / blk, 0.0).astype(BF16)
    km = jnp.dot(avg, k, preferred_element_type=F32)
    km_hi = km.astype(BF16)
    km_lo = (km - km_hi.astype(F32)).astype(BF16)

    q = q_ref[...]
    g2 = (lax.dot_general(km_hi, q, nt, preferred_element_type=F32)
          + lax.dot_general(km_lo, q, nt, preferred_element_type=F32))
    gate = g2[0:V7X_SUBLANES]
    row = lax.broadcasted_iota(jnp.int32, gate.shape, 0)
    own_blk = lax.broadcasted_iota(jnp.int32, gate.shape, 1) // blk
    gate = jnp.where(row < own_blk, gate, NEG)
    beaten = jnp.zeros(gate.shape, jnp.int32)
    for jp in range(nb):
        gj = gate[jp:jp + 1, :]
        beaten = beaten + ((gj > gate) | ((gj == gate) & (jp < row))).astype(jnp.int32)
    chosen = (beaten < MOBA_TOPK) & (row < own_blk)
    mask_t = jnp.where(chosen | (row == own_blk), 0.0, NEG)
    mask_t = jnp.concatenate([mask_t, jnp.zeros((V7X_LANES - V7X_SUBLANES, seq), F32)], axis=0)
    qx = jnp.concatenate([q, mask_t.T.astype(BF16)], axis=1)

    for own in range(nb):
        n = (own + 1) * blk
        s = lax.dot_general(qx[own * blk:n, :], kx_ref[0:n, :], nt, preferred_element_type=F32)
        parts = [s[:, n - blk:] + bias_ref[0]]
        if own >= 1:
            parts.insert(0, s[:, n - 2 * blk:n - blk] + bias_ref[1])
        if own >= 2:
            parts.insert(0, s[:, :n - 2 * blk])
        s = jnp.concatenate(parts, axis=1) if len(parts) > 1 else parts[0]
        m = jnp.max(s, axis=-1, keepdims=True)
        p = jnp.exp2(s - m)
        l = jnp.sum(p, axis=-1, keepdims=True)
        pv = jnp.dot(p.astype(BF16), v_ref[0:n, :], preferred_element_type=F32)
        o_ref[own * blk:n, :] = (pv * (1.0 / l)).astype(o_ref.dtype)


def moba_prompt_attn(q, k, v, bias, *, n_seq, seq):
    M, HD = q.shape
    hd = HD // N_HEADS
    nb = seq // MOBA_BLOCK
    assert seq % MOBA_BLOCK == 0 and nb <= V7X_SUBLANES
    head_spec = pl.BlockSpec((seq, hd), lambda b, h: (b, h))
    return pl.pallas_call(
        functools.partial(_moba_prompt_body, nb=nb), grid=(n_seq, N_HEADS),
        in_specs=[head_spec, head_spec, head_spec,
                  pl.BlockSpec((None, 2, MOBA_BLOCK, MOBA_BLOCK), lambda b, h: (h, 0, 0, 0))],
        out_specs=head_spec,
        out_shape=jax.ShapeDtypeStruct((M, HD), BF16),
        scratch_shapes=[pltpu.VMEM((seq, hd + V7X_LANES), BF16)],
        name="moba_prompt_attn",
        compiler_params=pltpu.CompilerParams(dimension_semantics=("arbitrary", "arbitrary"),
                                             vmem_limit_bytes=_vmem_limit(8 * MOBA_BLOCK * seq * 4)),
    )(q, k, v, bias)


def _gate_topk_body(q_ref, sums_ref, knew_ref, o_ref, *, nbp, ppb, own, inv_rows):
    b = pl.program_id(0)
    HD = q_ref.shape[-1]
    hd = HD // N_HEADS
    q = q_ref[pl.ds(b, 1), :]
    r = lax.broadcasted_iota(jnp.int32, (HD, V7X_LANES), 0)
    c = lax.broadcasted_iota(jnp.int32, (HD, V7X_LANES), 1)
    ind = (r // hd == c).astype(BF16)

    def head_sums(prod):
        hi = prod.astype(BF16)
        mid = (prod - hi.astype(F32)).astype(BF16)
        lo = (prod - hi.astype(F32) - mid.astype(F32)).astype(BF16)
        return (jnp.dot(hi, ind, preferred_element_type=F32) + jnp.dot(mid, ind, preferred_element_type=F32)
                + jnp.dot(lo, ind, preferred_element_type=F32))

    block_sums = sums_ref[:, 0:HD]
    for p in range(1, ppb):
        block_sums = block_sums + sums_ref[:, p * HD:(p + 1) * HD]
    means = block_sums * inv_rows
    gate_past = head_sums(means * q)
    k_new = jnp.broadcast_to(knew_ref[pl.ds(b, 1), :], (V7X_SUBLANES, HD))
    gate_rec = head_sums((k_new * inv_rows) * q)
    gate = jnp.concatenate([gate_past, gate_rec], axis=0)
    idx = lax.broadcasted_iota(jnp.int32, gate.shape, 0)
    gate = jnp.where(idx < own, gate, NEG)
    gate = jnp.where(idx <= nbp, gate, -jnp.inf)
    picks = []
    for _ in range(MOBA_TOPK):
        mx = jnp.max(gate, axis=0, keepdims=True)
        pick = jnp.min(jnp.where(gate == mx, idx, nbp + V7X_SUBLANES), axis=0, keepdims=True)
        picks.append(pick)
        gate = jnp.where(idx == pick, -jnp.inf, gate)
    picks.append(jnp.zeros((V7X_SUBLANES - MOBA_TOPK, V7X_LANES), jnp.int32))
    o_ref[...] = jnp.concatenate(picks, axis=0)


def moba_gate_topk(q_rows, page_sums, k_new_rows, *, n_seq, ppb):
    R, HD = q_rows.shape
    _, nbp, _ = page_sums.shape
    return pl.pallas_call(
        functools.partial(_gate_topk_body, nbp=nbp, ppb=ppb, own=nbp, inv_rows=1.0 / MOBA_BLOCK), grid=(n_seq,),
        in_specs=[
            pl.BlockSpec((R, HD), lambda b: (0, 0)),
            pl.BlockSpec((None, nbp, ppb * HD), lambda b: (b, 0, 0)),
            pl.BlockSpec((R, HD), lambda b: (0, 0)),
        ],
        out_specs=pl.BlockSpec((None, V7X_SUBLANES, V7X_LANES), lambda b: (b, 0, 0)),
        out_shape=jax.ShapeDtypeStruct((n_seq, V7X_SUBLANES, V7X_LANES), jnp.int32), name="moba_gate_topk",
        compiler_params=pltpu.CompilerParams(dimension_semantics=("arbitrary",)),
    )(q_rows, page_sums, k_new_rows)


DEC_HEADS_PER_STEP = 4


def _decode_attn_body(sel_ref, pt_ref, table_ref, q_ref, kn_ref, vn_ref, ck_ref, cv_ref, o_ref,
                      kbuf, vbuf, sem, *, nbp, ppb, t_new, scale):
    step = pl.program_id(0)
    hd = kbuf.shape[-1]
    hg = DEC_HEADS_PER_STEP
    groups = N_HEADS // hg
    n_buf = MOBA_TOPK * ppb

    def page_copies(st, slot):
        b = st // groups
        out = []
        for hh in range(hg):
            h = (st % groups) * hg + hh
            for kk in range(MOBA_TOPK):
                blk = jnp.minimum(sel_ref[b, h, kk], nbp - 1)
                for p in range(ppb):
                    page = pt_ref[b, blk * ppb + p]
                    i = hh * n_buf + kk * ppb + p
                    out.append(pltpu.make_async_copy(ck_ref.at[page, :, h, :], kbuf.at[slot, i], sem.at[0, slot, i]))
                    out.append(pltpu.make_async_copy(cv_ref.at[page, :, h, :], vbuf.at[slot, i], sem.at[1, slot, i]))
        return out

    slot = step % 2

    @pl.when(step == 0)
    def _():
        for c in page_copies(step, slot):
            c.start()

    @pl.when(step + 1 < pl.num_programs(0))
    def _():
        for c in page_copies(step + 1, 1 - slot):
            c.start()

    for c in page_copies(step, slot):
        c.wait()

    b = step // groups
    nt = (((1,), (1,)), ((), ()))
    for hh in range(hg):
        h = (step % groups) * hg + hh
        lanes = pl.ds(pl.multiple_of(h * hd, hd), hd)

        def rows8(ref):
            x = ref[:, lanes]
            r = lax.broadcasted_iota(jnp.int32, x.shape, 0)
            row = jnp.sum(jnp.where(r == b, x, 0.0), axis=0, keepdims=True)
            return jnp.broadcast_to(row, (V7X_SUBLANES, hd))

        q = rows8(q_ref)
        qb = q.astype(BF16)
        table_at = lambda k, h=h: table_ref[k, h]
        kn, vn = rows8(kn_ref), rows8(vn_ref)
        l_new = jnp.sum(q * kn, axis=-1, keepdims=True) * scale + table_at(0)
        logits, values = [], []
        for kk in range(MOBA_TOPK):
            sel = sel_ref[b, h, kk]
            valid = sel < nbp
            selp = jnp.minimum(sel, nbp - 1)
            for p in range(ppb):
                i = hh * n_buf + kk * ppb + p
                kp = kbuf[slot, i].astype(BF16)
                s = lax.dot_general(qb, kp, nt, preferred_element_type=F32) * scale
                pos = selp * MOBA_BLOCK + p * PAGE_SIZE + lax.broadcasted_iota(jnp.int32, s.shape, 1)
                s = s + _bias_from_distance(t_new - pos, table_at)
                logits.append(jnp.where(valid, s, NEG))
                values.append(vbuf[slot, i].astype(BF16))
        m = l_new
        for s in logits:
            m = jnp.maximum(m, jnp.max(s, axis=-1, keepdims=True))
        p_new = jnp.exp(l_new - m)
        den = p_new
        acc = p_new * vn
        for s, v in zip(logits, values):
            p = jnp.exp(s - m)
            den = den + jnp.sum(p, axis=-1, keepdims=True)
            acc = acc + jnp.dot(p.astype(BF16), v, preferred_element_type=F32)
        o_ref[hh] = acc / den


def moba_decode_attn(sel, page_table, rel_bias, q_rows, k_new_rows, v_new_rows, cache_k, cache_v, *, n_seq):
    R, HD = q_rows.shape
    hd = HD // N_HEADS
    n_pages = page_table.shape[1]
    ppb = MOBA_BLOCK // PAGE_SIZE
    nbp = n_pages // ppb
    hg = DEC_HEADS_PER_STEP
    assert n_pages % ppb == 0, "past length must be a whole number of MoBA blocks"
    assert N_HEADS % hg == 0
    row_spec = pl.BlockSpec((R, HD), lambda s, sel, pt: (0, 0))
    n_buf = hg * MOBA_TOPK * ppb
    body = functools.partial(_decode_attn_body, nbp=nbp, ppb=ppb, t_new=n_pages * PAGE_SIZE, scale=hd ** -0.5)
    return pl.pallas_call(
        body,
        grid_spec=pltpu.PrefetchScalarGridSpec(
            num_scalar_prefetch=2, grid=(n_seq * N_HEADS // hg,),
            in_specs=[pl.BlockSpec(memory_space=pltpu.SMEM), row_spec, row_spec, row_spec,
                      pl.BlockSpec(memory_space=pl.ANY), pl.BlockSpec(memory_space=pl.ANY)],
            out_specs=pl.BlockSpec((hg, V7X_SUBLANES, hd), lambda s, sel, pt: (s, 0, 0)),
            scratch_shapes=[pltpu.VMEM((2, n_buf, PAGE_SIZE, hd), F32), pltpu.VMEM((2, n_buf, PAGE_SIZE, hd), F32),
                            pltpu.SemaphoreType.DMA((2, 2, n_buf))],
        ),
        out_shape=jax.ShapeDtypeStruct((n_seq * N_HEADS, V7X_SUBLANES, hd), F32), name="moba_decode_attn",
        compiler_params=pltpu.CompilerParams(dimension_semantics=("arbitrary",)),
    )(sel, page_table, rel_bias, q_rows, k_new_rows, v_new_rows, cache_k, cache_v)


ROW_TILE = 1024
DEC_ROWS = 16
CONV_TIME_TILE = 256


def kernel(x_prompt, x_sample, state_conv, cache_k, cache_v, page_table, norm_mix_g, norm_ffn_g, a_w_in, a_b_in, a_w_dw, a_b_dw, a_ln_g, a_ln_b, a_w_out, a_b_out, kv_norm_g, w_k, w_v, b_w_q, b_w_o, rel_bias, f_w_gate, f_w_up, f_w_down, final_norm_g):
    B, S, D = x_prompt.shape
    DB, dec_seq, _ = x_sample.shape
    assert dec_seq == 1, "decode path handles one new token per sequence"
    n_a = a_w_in.shape[0]
    depth = norm_mix_g.shape[0]
    F = f_w_gate.shape[-1]
    hd = D // N_HEADS
    ppb = MOBA_BLOCK // PAGE_SIZE
    n_pages = page_table.shape[1]

    bf = lambda w: w.astype(BF16)
    a_w_in_b, a_w_out_b = bf(a_w_in), bf(a_w_out)
    w_k_b, w_v_b, w_q_b, w_o_b = bf(w_k)[None], bf(w_v)[None], bf(b_w_q), bf(b_w_o)
    w_gate_b, w_up_b, w_down_b = bf(f_w_gate), bf(f_w_up), bf(f_w_down)

    xp = x_prompt.reshape(B * S, D)
    xs = jnp.pad(x_sample.reshape(DB, D), ((0, DEC_ROWS - DB), (0, 0)))
    streams = [dict(x=xp, tm=ROW_TILE, tag="p"), dict(x=xs, tm=DEC_ROWS, tag="s")]
    mm = lambda st, *a, **kw: fused_matmul(*a, tm=st["tm"], **kw)

    conv_p, conv_s = [], []
    bias_tables = relbias_tables(rel_bias)
    page_ids = page_table.reshape(-1)
    ids_per_layer = -(-page_ids.shape[0] // n_a)
    page_sum_parts = []
    for l in range(depth):
        for st in streams:
            x = st["x"]
            if l < n_a:
                b_in = a_b_in[l].reshape(1, 2 * D)
                u = mm(st, x, [(a_w_in_b, l, 0), (a_w_in_b, l, D)], D, gain=norm_mix_g[l],
                       biases=[(b_in, 0), (b_in, D)], epilogue="glu", name="glu_" + st["tag"])
                if st["tag"] == "p":
                    u3 = u.reshape(B, S, D)
                    left = jnp.zeros((B, CONV_WIDTH - 1, D), F32)
                    c = conv_ln_silu(u3, left, a_w_dw[l], a_b_dw[l], a_ln_g[l], a_ln_b[l],
                                     tt=CONV_TIME_TILE).reshape(B * S, D)
                    conv_p.append(u3[:, S - (CONV_WIDTH - 1):, :])
                else:
                    c = conv_step(state_conv[l], u[:DB], a_w_dw[l], a_b_dw[l], a_ln_g[l], a_ln_b[l])
                    c = jnp.pad(c, ((0, DEC_ROWS - DB), (0, 0)))
                    conv_s.append(jnp.concatenate([state_conv[l][:, 1:, :], u[:DB, None, :]], axis=1))
                x = mm(st, c, [(a_w_out_b, l, 0)], D, biases=[(a_b_out[l].reshape(1, D), 0)], residual=x,
                       name="convout_" + st["tag"])
            else:
                j = l - n_a
                if l == n_a:
                    kv_outs = ((F32, hd), (BF16, None)) if st["tag"] == "p" else ((F32, None),)
                    st["k"] = mm(st, x, [(w_k_b, 0, 0)], D, gain=kv_norm_g, outs=kv_outs, name="kproj_" + st["tag"])
                    st["v"] = mm(st, x, [(w_v_b, 0, 0)], D, gain=kv_norm_g, outs=kv_outs, name="vproj_" + st["tag"])
                if st["tag"] == "p":
                    q = mm(st, x, [(w_q_b, j, 0)], D, gain=norm_mix_g[l], outs=((BF16, None),),
                           out_scale=hd ** -0.5 * LOG2E, name="qproj_p")
                    o = moba_prompt_attn(q, st["k"][1], st["v"][1], bias_tables, n_seq=B, seq=S)
                else:
                    q = mm(st, x, [(w_q_b, j, 0)], D, gain=norm_mix_g[l], name="qproj_s")
                    page_sums = jnp.concatenate(page_sum_parts).reshape(DB, n_pages // ppb, ppb * D)
                    picks = moba_gate_topk(q, page_sums, st["k"], n_seq=DB, ppb=ppb)
                    sel = picks[:, :MOBA_TOPK, :N_HEADS].transpose(0, 2, 1)
                    o8 = moba_decode_attn(sel, page_table, rel_bias, q, st["k"], st["v"], cache_k, cache_v,
                                          n_seq=DB)
                    o = jnp.pad(o8[:, 0, :].reshape(DB, D), ((0, DEC_ROWS - DB), (0, 0))).astype(BF16)
                x = mm(st, o, [(w_o_b, j, 0)], D, residual=x, name="oproj_" + st["tag"])
            ids = page_ids[l * ids_per_layer:(l + 1) * ids_per_layer] if (st["tag"] == "p" and l < n_a) else None
            act = mm(st, x, [(w_gate_b, l, 0), (w_up_b, l, 0)], F, gain=norm_ffn_g[l], epilogue="swiglu",
                     outs=((BF16, None),), page_stream=None if ids is None else (cache_k, ids),
                     name="ffn_up_" + st["tag"])
            if ids is not None:
                act, sums = act
                page_sum_parts.append(sums[:ids.shape[0]])
            x = mm(st, act, [(w_down_b, l, 0)], D, residual=x, name="ffn_down_" + st["tag"])
            st["x"] = x

    y_prompt = rmsnorm_rows(streams[0]["x"], final_norm_g, tm=ROW_TILE // 2).reshape(B, S, D)
    y_sample = rmsnorm_rows(streams[1]["x"], final_norm_g, tm=DEC_ROWS)[:DB].reshape(DB, 1, D)
    k_s, v_s = streams[1]["k"], streams[1]["v"]
    return (y_prompt, y_sample, jnp.stack(conv_p), jnp.stack(conv_s),
            streams[0]["k"][0].reshape(B, S, N_HEADS, hd), streams[0]["v"][0].reshape(B, S, N_HEADS, hd),
            k_s[:DB].reshape(DB, 1, N_HEADS, hd), v_s[:DB].reshape(DB, 1, N_HEADS, hd))
```

```python
import functools
import math

import numpy as np
import jax
import jax.numpy as jnp
from jax import lax
from jax.experimental import pallas as pl
from jax.experimental.pallas import tpu as pltpu

F32 = jnp.float32
BF16 = jnp.bfloat16

N_HEADS = 16
CONV_WIDTH = 31
MOBA_BLOCK = 256
MOBA_TOPK = 3
NUM_BUCKETS = 32
MAX_DISTANCE = 128
PAGE_SIZE = 128
EPS = 1e-6
NEG = -1e30
LOG2E = math.log2(math.e)

V7X_VMEM_BYTES = 64 * 1024 * 1024
V7X_LANES = 128
V7X_SUBLANES = 8
VMEM_CAP = V7X_VMEM_BYTES - 4 * 1024 * 1024


def _vmem_limit(est_bytes):
    return int(min(max(est_bytes + (8 << 20), 16 << 20), VMEM_CAP))


def _bucket_thresholds():
    max_exact = NUM_BUCKETS // 2
    n = np.arange(0, 4 * MAX_DISTANCE, dtype=np.int64)
    nf = np.maximum(n, max_exact).astype(np.float32)
    large = max_exact + (np.log(nf / np.float32(max_exact)) / np.float32(math.log(MAX_DISTANCE / max_exact))
                         * np.float32(NUM_BUCKETS - max_exact)).astype(np.int32)
    large = np.minimum(large, NUM_BUCKETS - 1)
    bucket = np.where(n < max_exact, n, large)
    assert np.all(np.diff(bucket) >= 0) and bucket[-1] == NUM_BUCKETS - 1
    return [int(np.argmax(bucket >= k)) for k in range(1, NUM_BUCKETS)]


_BUCKET_START = _bucket_thresholds()


def _bias_from_distance(dist, table_at):
    bias = jnp.full(dist.shape, table_at(0), F32)
    for k in range(1, NUM_BUCKETS):
        bias = jnp.where(dist >= _BUCKET_START[k - 1], table_at(k), bias)
    return bias


def _mm_body(*refs, has_norm, n_w, has_bias, has_res, has_side, epilogue, out_scales, n_outs, n_pages):
    it = iter(refs)
    if n_pages:
        next(it)
    x_refs = [next(it) for _ in range(1 + has_side)]
    g_ref = next(it) if has_norm else None
    w_refs = [next(it) for _ in range(n_w)]
    b_refs = [next(it) for _ in range(n_w)] if has_bias else []
    res_refs = [next(it) for _ in range(1 + has_side)] if has_res else [None, None]
    page_refs = [next(it) for _ in range(n_pages)]
    out_refs = [[next(it) for _ in range(n)] for n in n_outs]
    page_out = next(it) if n_pages else None
    if has_norm:
        xn_refs = [next(it) for _ in x_refs]

        @pl.when(pl.program_id(1) == 0)
        def _():
            for x_ref, xn_ref in zip(x_refs, xn_refs):
                x = x_ref[...]
                ms = jnp.mean(x * x, axis=-1, keepdims=True)
                xn_ref[...] = (x * lax.rsqrt(ms + EPS) * g_ref[...]).astype(xn_ref.dtype)

        x_refs = xn_refs

    def stream(x_ref, res_ref, outs, out_scale):
        xb = x_ref[...]
        accs = [jnp.dot(xb, w[...], preferred_element_type=F32) for w in w_refs]
        if has_bias:
            accs = [a + b[...] for a, b in zip(accs, b_refs)]
        if epilogue == "glu":
            y = accs[0] * jax.nn.sigmoid(accs[1])
        elif epilogue == "swiglu":
            y = jax.nn.silu(accs[0]) * accs[1]
        else:
            y = accs[0]
        if out_scale is not None:
            y = y * out_scale
        if has_res:
            y = res_ref[...] + y
        for o in outs:
            if len(o.shape) == 3:
                hd = o.shape[-1]
                for h in range(o.shape[1]):
                    o[:, h, :] = y[:, h * hd:(h + 1) * hd].astype(o.dtype)
            else:
                o[...] = y.astype(o.dtype)

    stream(x_refs[0], res_refs[0], out_refs[0], out_scales[0])
    if has_side:
        @pl.when(pl.program_id(0) == 0)
        def _():
            stream(x_refs[1], res_refs[1], out_refs[1], out_scales[1])

    for k, page_ref in enumerate(page_refs):
        page_out[k] = jnp.sum(page_ref[...], axis=0)


MATMUL_VMEM_BUDGET = 48 * 1024 * 1024


def fused_matmul(x, ws, n_cols, *, gain=None, biases=None, residual=None, epilogue="none",
                 outs=((F32, None),), out_scale=None, side=None, page_stream=None, tm, name):
    M, K = x.shape
    has_norm = gain is not None
    has_side = side is not None
    out_bytes = sum(jnp.dtype(dt).itemsize for dt, _ in outs) + (4 if residual is not None else 0)

    def pages_per_step(tn):
        return -(-page_stream[1].shape[0] // ((M // tm) * (n_cols // tn))) if page_stream is not None else 0

    def estimate(tn):
        page_bytes = 2 * pages_per_step(tn) * math.prod(page_stream[0].shape[1:]) * 4 if page_stream else 0
        return (2 * tm * K * x.dtype.itemsize + (tm * K * 2 if has_norm else 0) + 2 * len(ws) * K * tn * 2
                + 2 * tm * tn * out_bytes + (len(ws) + 1) * tm * tn * 4 + page_bytes)

    def legal(tn):
        heads_ok = all(hd is None or (tn // hd) % V7X_SUBLANES == 0 or tn == n_cols for _, hd in outs)
        offs_ok = all(off % tn == 0 for _, _, off in ws)
        return n_cols % tn == 0 and heads_ok and offs_ok

    tiles = [t for t in (2048, 1024, 512, 256, 128) if legal(t)]
    fitting = [t for t in tiles if estimate(t) <= MATMUL_VMEM_BUDGET]
    tn = fitting[0] if fitting else tiles[-1]
    assert M % tm == 0
    nj = n_cols // tn
    grid = (M // tm, nj)
    in_specs = [pl.BlockSpec((tm, K), lambda i, j, *_: (i, 0))]
    args = [x]
    if has_side:
        x_side, res_side, side_dtypes, side_scale = side
        R = x_side.shape[0]
        side_block = lambda i, j, *_: (0, jnp.where(i == 0, j, nj - 1))
        in_specs.append(pl.BlockSpec((R, K), lambda i, j, *_: (0, 0)))
        args.append(x_side)
    if has_norm:
        in_specs.append(pl.BlockSpec((1, K), lambda i, j, *_: (0, 0)))
        args.append(gain.reshape(1, K))
    for w, layer, off in ws:
        in_specs.append(pl.BlockSpec((None, K, tn), lambda i, j, *_, layer=layer, ob=off // tn: (layer, 0, j + ob)))
        args.append(w)
    if biases is not None:
        for b, off in biases:
            in_specs.append(pl.BlockSpec((1, tn), lambda i, j, *_, ob=off // tn: (0, j + ob)))
            args.append(b)
    if residual is not None:
        in_specs.append(pl.BlockSpec((tm, tn), lambda i, j, *_: (i, j)))
        args.append(residual)
        if has_side:
            in_specs.append(pl.BlockSpec((R, tn), side_block))
            args.append(res_side)
    out_specs, out_shape = [], []
    for dt, hd in outs:
        if hd is None:
            out_specs.append(pl.BlockSpec((tm, tn), lambda i, j, *_: (i, j)))
            out_shape.append(jax.ShapeDtypeStruct((M, n_cols), dt))
        else:
            out_specs.append(pl.BlockSpec((tm, tn // hd, hd), lambda i, j, *_: (i, j, 0)))
            out_shape.append(jax.ShapeDtypeStruct((M, n_cols // hd, hd), dt))
    n_outs, out_scales = [len(outs)], [out_scale]
    if has_side:
        for dt in side_dtypes:
            out_specs.append(pl.BlockSpec((R, tn), side_block))
            out_shape.append(jax.ShapeDtypeStruct((R, n_cols), dt))
        n_outs.append(len(side_dtypes))
        out_scales.append(side_scale)
    prefetch = []
    per_step = pages_per_step(tn)
    if page_stream is not None:
        cache, ids = page_stream
        n_slots = grid[0] * nj * per_step
        assert ids.shape[0] <= n_slots
        prefetch = [jnp.concatenate([ids, jnp.broadcast_to(ids[-1:], (n_slots - ids.shape[0],))])]
        for k in range(per_step):
            in_specs.append(pl.BlockSpec(
                (None,) + cache.shape[1:],
                lambda i, j, ids_ref, k=k: (ids_ref[(i * nj + j) * per_step + k], 0, 0, 0)))
            args.append(cache)
        out_specs.append(pl.BlockSpec((per_step,) + cache.shape[2:], lambda i, j, *_: (i * nj + j, 0, 0)))
        out_shape.append(jax.ShapeDtypeStruct((n_slots,) + cache.shape[2:], F32))
    scratch = []
    if has_norm:
        scratch = [pltpu.VMEM((tm, K), BF16)] + ([pltpu.VMEM((R, K), BF16)] if has_side else [])
    body = functools.partial(_mm_body, has_norm=has_norm, n_w=len(ws), has_bias=biases is not None,
                             has_res=residual is not None, has_side=has_side, epilogue=epilogue,
                             out_scales=out_scales, n_outs=n_outs, n_pages=per_step)
    res = pl.pallas_call(
        body,
        grid_spec=pltpu.PrefetchScalarGridSpec(num_scalar_prefetch=len(prefetch), grid=grid, in_specs=in_specs,
                                               out_specs=out_specs, scratch_shapes=scratch),
        out_shape=out_shape, name=name,
        compiler_params=pltpu.CompilerParams(dimension_semantics=("arbitrary", "arbitrary"),
                                             vmem_limit_bytes=_vmem_limit(estimate(tn))),
    )(*prefetch, *args)
    return res if len(res) > 1 else res[0]


def _rmsnorm_body(x_ref, g_ref, o_ref):
    x = x_ref[...]
    ms = jnp.mean(x * x, axis=-1, keepdims=True)
    o_ref[...] = x * lax.rsqrt(ms + EPS) * g_ref[...]


def rmsnorm_rows(x, gain, *, tm):
    M, D = x.shape
    return pl.pallas_call(
        _rmsnorm_body, grid=(M // tm,),
        in_specs=[pl.BlockSpec((tm, D), lambda i: (i, 0)), pl.BlockSpec((1, D), lambda i: (0, 0))],
        out_specs=pl.BlockSpec((tm, D), lambda i: (i, 0)),
        out_shape=jax.ShapeDtypeStruct((M, D), F32), name="rmsnorm_rows",
        compiler_params=pltpu.CompilerParams(dimension_semantics=("arbitrary",),
                                             vmem_limit_bytes=_vmem_limit(4 * tm * D * 4)),
    )(x, gain.reshape(1, D))


CONV_HALO = 32
CONV_ROWS = 64


def _ln_silu(c, g, b):
    mu = jnp.mean(c, axis=-1, keepdims=True)
    xc = c - mu
    var = jnp.mean(xc * xc, axis=-1, keepdims=True)
    y = xc * lax.rsqrt(var + EPS) * g + b
    return y * jax.nn.sigmoid(y)


def _conv_body(u_ref, halo_ref, left_ref, w_ref, bdw_ref, g_ref, b_ref, o_ref, ext_ref, y_ref, *, tt):
    first = pl.program_id(1) == 0
    ext_ref[0:CONV_HALO, :] = jnp.where(first, left_ref[...], halo_ref[...])
    ext_ref[CONV_HALO:, :] = u_ref[...]
    d_model = u_ref.shape[-1]
    lead = CONV_HALO - (CONV_WIDTH - 1)

    def chunk(c, carry):
        lanes = pl.ds(pl.multiple_of(c * V7X_LANES, V7X_LANES), V7X_LANES)
        w = w_ref[:, lanes]
        for r0 in range(0, tt, CONV_ROWS):
            n_win = CONV_ROWS + CONV_HALO
            win = ext_ref[r0:r0 + n_win, lanes]
            acc = jnp.zeros((CONV_ROWS, V7X_LANES), F32)
            for res in range(V7X_SUBLANES):
                shifted = win if res == 0 else pltpu.roll(win, n_win - res, axis=0)
                for off in range(res, lead + CONV_WIDTH, V7X_SUBLANES):
                    k = off - lead
                    if k >= 0:
                        acc = acc + shifted[off - res:off - res + CONV_ROWS, :] * w[k:k + 1, :]
            y_ref[r0:r0 + CONV_ROWS, lanes] = acc
        return carry

    lax.fori_loop(0, d_model // V7X_LANES, chunk, 0)
    c = y_ref[...] + bdw_ref[...]
    o_ref[...] = _ln_silu(c, g_ref[...], b_ref[...]).astype(o_ref.dtype)


def conv_ln_silu(u, left, w_dw, b_dw, ln_g, ln_b, *, tt):
    B, T, D = u.shape
    left_pad = jnp.pad(left, ((0, 0), (CONV_HALO - (CONV_WIDTH - 1), 0), (0, 0)))
    hb = tt // CONV_HALO
    row = lambda a: a.reshape(1, D)
    est = 2 * tt * D * 4 + 4 * CONV_HALO * D * 4 + (2 * tt + CONV_HALO) * D * 4 + 2 * tt * D * 2 + 6 * tt * D * 4
    return pl.pallas_call(
        functools.partial(_conv_body, tt=tt), grid=(B, T // tt),
        in_specs=[
            pl.BlockSpec((None, tt, D), lambda b, i: (b, i, 0)),
            pl.BlockSpec((None, CONV_HALO, D), lambda b, i: (b, jnp.maximum(i * hb - 1, 0), 0)),
            pl.BlockSpec((None, CONV_HALO, D), lambda b, i: (b, 0, 0)),
            pl.BlockSpec((CONV_WIDTH, D), lambda b, i: (0, 0)),
            pl.BlockSpec((1, D), lambda b, i: (0, 0)),
            pl.BlockSpec((1, D), lambda b, i: (0, 0)),
            pl.BlockSpec((1, D), lambda b, i: (0, 0)),
        ],
        out_specs=pl.BlockSpec((None, tt, D), lambda b, i: (b, i, 0)),
        out_shape=jax.ShapeDtypeStruct((B, T, D), BF16),
        scratch_shapes=[pltpu.VMEM((tt + CONV_HALO, D), F32), pltpu.VMEM((tt, D), F32)],
        name="conv_ln_silu",
        compiler_params=pltpu.CompilerParams(dimension_semantics=("arbitrary", "arbitrary"),
                                             vmem_limit_bytes=_vmem_limit(est)),
    )(u, u, left_pad, w_dw, row(b_dw), row(ln_g), row(ln_b))


def _conv_step_body(state_ref, u_ref, w_ref, bdw_ref, g_ref, b_ref, o_ref):
    acc = u_ref[...] * w_ref[CONV_WIDTH - 1:CONV_WIDTH, :]
    for k in range(CONV_WIDTH - 1):
        acc = acc + state_ref[:, k, :] * w_ref[k:k + 1, :]
    c = acc + bdw_ref[...]
    o_ref[...] = _ln_silu(c, g_ref[...], b_ref[...]).astype(o_ref.dtype)


def conv_step(state, u_new, w_dw, b_dw, ln_g, ln_b):
    NB, _, D = state.shape
    row = lambda a: a.reshape(1, D)
    return pl.pallas_call(
        _conv_step_body,
        out_shape=jax.ShapeDtypeStruct((NB, D), BF16), name="conv_step",
    )(state, u_new, w_dw, row(b_dw), row(ln_g), row(ln_b))


def _relbias_body(table_ref, o_ref):
    h = pl.program_id(0)
    a = lax.broadcasted_iota(jnp.int32, (MOBA_BLOCK, MOBA_BLOCK), 0)
    b = lax.broadcasted_iota(jnp.int32, (MOBA_BLOCK, MOBA_BLOCK), 1)
    far = table_ref[NUM_BUCKETS - 1, h]
    table_at = lambda k: table_ref[k, h]
    d_own = a - b
    own = (_bias_from_distance(d_own, table_at) - far) * LOG2E
    o_ref[0] = jnp.where(d_own >= 0, own, NEG)
    o_ref[1] = (_bias_from_distance(d_own + MOBA_BLOCK, table_at) - far) * LOG2E


def relbias_tables(rel_bias):
    return pl.pallas_call(
        _relbias_body, grid=(N_HEADS,),
        in_specs=[pl.BlockSpec(memory_space=pltpu.SMEM)],
        out_specs=pl.BlockSpec((None, 2, MOBA_BLOCK, MOBA_BLOCK), lambda h: (h, 0, 0, 0)),
        out_shape=jax.ShapeDtypeStruct((N_HEADS, 2, MOBA_BLOCK, MOBA_BLOCK), F32), name="relbias_tables",
    )(rel_bias)


GATE_ROWS = 16


def _moba_prompt_body(q_ref, k_ref, v_ref, bias_ref, o_ref, kx_ref, *, nb):
    blk = MOBA_BLOCK
    hd = q_ref.shape[-1]
    seq = nb * blk
    nt = (((1,), (1,)), ((), ()))

    k = k_ref[...]
    kx_ref[:, 0:hd] = k
    kr = lax.broadcasted_iota(jnp.int32, (seq, V7X_LANES), 0)
    kc = lax.broadcasted_iota(jnp.int32, (seq, V7X_LANES), 1)
    kx_ref[:, hd:] = (kr // blk == kc).astype(BF16)
    r = lax.broadcasted_iota(jnp.int32, (GATE_ROWS, seq), 0)
    c = lax.broadcasted_iota(jnp.int32, (GATE_ROWS, seq), 1)
    avg = jnp.where(c // blk == r, 1.0 / blk, 0.0).astype(BF16)
    km = jnp.dot(avg, k, preferred_element_type=F32)
    km_hi = km.astype(BF16)
    km_lo = (km - km_hi.astype(F32)).astype(BF16)

    q = q_ref[...]
    g2 = (lax.dot_general(km_hi, q, nt, preferred_element_type=F32)
          + lax.dot_general(km_lo, q, nt, preferred_element_type=F32))
    gate = g2[0:V7X_SUBLANES]
    row = lax.broadcasted_iota(jnp.int32, gate.shape, 0)
    own_blk = lax.broadcasted_iota(jnp.int32, gate.shape, 1) // blk
    gate = jnp.where(row < own_blk, gate, NEG)
    beaten = jnp.zeros(gate.shape, jnp.int32)
    for jp in range(nb):
        gj = gate[jp:jp + 1, :]
        beaten = beaten + ((gj > gate) | ((gj == gate) & (jp < row))).astype(jnp.int32)
    chosen = (beaten < MOBA_TOPK) & (row < own_blk)
    mask_t = jnp.where(chosen | (row == own_blk), 0.0, NEG)
    mask_t = jnp.concatenate([mask_t, jnp.zeros((V7X_LANES - V7X_SUBLANES, seq), F32)], axis=0)
    qx = jnp.concatenate([q, mask_t.T.astype(BF16)], axis=1)

    for own in range(nb):
        n = (own + 1) * blk
        s = lax.dot_general(qx[own * blk:n, :], kx_ref[0:n, :], nt, preferred_element_type=F32)
        parts = [s[:, n - blk:] + bias_ref[0]]
        if own >= 1:
            parts.insert(0, s[:, n - 2 * blk:n - blk] + bias_ref[1])
        if own >= 2:
            parts.insert(0, s[:, :n - 2 * blk])
        s = jnp.concatenate(parts, axis=1) if len(parts) > 1 else parts[0]
        m = jnp.max(s, axis=-1, keepdims=True)
        p = jnp.exp2(s - m)
        l = jnp.sum(p, axis=-1, keepdims=True)
        pv = jnp.dot(p.astype(BF16), v_ref[0:n, :], preferred_element_type=F32)
        o_ref[own * blk:n, :] = (pv * (1.0 / l)).astype(o_ref.dtype)


def moba_prompt_attn(q, k, v, bias, *, n_seq, seq):
    M, HD = q.shape
    hd = HD // N_HEADS
    nb = seq // MOBA_BLOCK
    assert seq % MOBA_BLOCK == 0 and nb <= V7X_SUBLANES
    head_spec = pl.BlockSpec((seq, hd), lambda b, h: (b, h))
    return pl.pallas_call(
        functools.partial(_moba_prompt_body, nb=nb), grid=(n_seq, N_HEADS),
        in_specs=[head_spec, head_spec, head_spec,
                  pl.BlockSpec((None, 2, MOBA_BLOCK, MOBA_BLOCK), lambda b, h: (h, 0, 0, 0))],
        out_specs=head_spec,
        out_shape=jax.ShapeDtypeStruct((M, HD), BF16),
        scratch_shapes=[pltpu.VMEM((seq, hd + V7X_LANES), BF16)],
        name="moba_prompt_attn",
        compiler_params=pltpu.CompilerParams(dimension_semantics=("arbitrary", "arbitrary"),
                                             vmem_limit_bytes=_vmem_limit(8 * MOBA_BLOCK * seq * 4)),
    )(q, k, v, bias)


def _gate_topk_body(q_ref, sums_ref, knew_ref, o_ref, *, nbp, ppb, own, inv_rows):
    b = pl.program_id(0)
    HD = q_ref.shape[-1]
    hd = HD // N_HEADS
    q = q_ref[pl.ds(b, 1), :]
    r = lax.broadcasted_iota(jnp.int32, (HD, V7X_LANES), 0)
    c = lax.broadcasted_iota(jnp.int32, (HD, V7X_LANES), 1)
    ind = (r // hd == c).astype(BF16)

    def head_sums(prod):
        hi = prod.astype(BF16)
        mid = (prod - hi.astype(F32)).astype(BF16)
        lo = (prod - hi.astype(F32) - mid.astype(F32)).astype(BF16)
        return (jnp.dot(hi, ind, preferred_element_type=F32) + jnp.dot(mid, ind, preferred_element_type=F32)
                + jnp.dot(lo, ind, preferred_element_type=F32))

    block_sums = sums_ref[:, 0:HD]
    for p in range(1, ppb):
        block_sums = block_sums + sums_ref[:, p * HD:(p + 1) * HD]
    means = block_sums * inv_rows
    gate_past = head_sums(means * q)
    k_new = jnp.broadcast_to(knew_ref[pl.ds(b, 1), :], (V7X_SUBLANES, HD))
    gate_rec = head_sums((k_new * inv_rows) * q)
    gate = jnp.concatenate([gate_past, gate_rec], axis=0)
    idx = lax.broadcasted_iota(jnp.int32, gate.shape, 0)
    gate = jnp.where(idx < own, gate, NEG)
    gate = jnp.where(idx <= nbp, gate, -jnp.inf)
    picks = []
    for _ in range(MOBA_TOPK):
        mx = jnp.max(gate, axis=0, keepdims=True)
        pick = jnp.min(jnp.where(gate == mx, idx, nbp + V7X_SUBLANES), axis=0, keepdims=True)
        picks.append(pick)
        gate = jnp.where(idx == pick, -jnp.inf, gate)
    picks.append(jnp.zeros((V7X_SUBLANES - MOBA_TOPK, V7X_LANES), jnp.int32))
    o_ref[...] = jnp.concatenate(picks, axis=0)


def moba_gate_topk(q_rows, page_sums, k_new_rows, *, n_seq, ppb):
    R, HD = q_rows.shape
    _, nbp, _ = page_sums.shape
    return pl.pallas_call(
        functools.partial(_gate_topk_body, nbp=nbp, ppb=ppb, own=nbp, inv_rows=1.0 / MOBA_BLOCK), grid=(n_seq,),
        in_specs=[
            pl.BlockSpec((R, HD), lambda b: (0, 0)),
            pl.BlockSpec((None, nbp, ppb * HD), lambda b: (b, 0, 0)),
            pl.BlockSpec((R, HD), lambda b: (0, 0)),
        ],
        out_specs=pl.BlockSpec((None, V7X_SUBLANES, V7X_LANES), lambda b: (b, 0, 0)),
        out_shape=jax.ShapeDtypeStruct((n_seq, V7X_SUBLANES, V7X_LANES), jnp.int32), name="moba_gate_topk",
        compiler_params=pltpu.CompilerParams(dimension_semantics=("arbitrary",)),
    )(q_rows, page_sums, k_new_rows)


DEC_HEADS_PER_STEP = 4


def _decode_attn_body(sel_ref, pt_ref, table_ref, q_ref, kn_ref, vn_ref, ck_ref, cv_ref, o_ref,
                      kbuf, vbuf, sem, *, nbp, ppb, t_new, scale):
    step = pl.program_id(0)
    hd = kbuf.shape[-1]
    hg = DEC_HEADS_PER_STEP
    groups = N_HEADS // hg
    n_buf = MOBA_TOPK * ppb

    def page_copies(st, slot):
        b = st // groups
        out = []
        for hh in range(hg):
            h = (st % groups) * hg + hh
            for kk in range(MOBA_TOPK):
                blk = jnp.minimum(sel_ref[b, h, kk], nbp - 1)
                for p in range(ppb):
                    page = pt_ref[b, blk * ppb + p]
                    i = hh * n_buf + kk * ppb + p
                    out.append(pltpu.make_async_copy(ck_ref.at[page, :, h, :], kbuf.at[slot, i], sem.at[0, slot, i]))
                    out.append(pltpu.make_async_copy(cv_ref.at[page, :, h, :], vbuf.at[slot, i], sem.at[1, slot, i]))
        return out

    slot = step % 2

    @pl.when(step == 0)
    def _():
        for c in page_copies(step, slot):
            c.start()

    @pl.when(step + 1 < pl.num_programs(0))
    def _():
        for c in page_copies(step + 1, 1 - slot):
            c.start()

    for c in page_copies(step, slot):
        c.wait()

    b = step // groups
    nt = (((1,), (1,)), ((), ()))
    for hh in range(hg):
        h = (step % groups) * hg + hh
        lanes = pl.ds(pl.multiple_of(h * hd, hd), hd)

        def rows8(ref):
            x = ref[:, lanes]
            r = lax.broadcasted_iota(jnp.int32, x.shape, 0)
            row = jnp.sum(jnp.where(r == b, x, 0.0), axis=0, keepdims=True)
            return jnp.broadcast_to(row, (V7X_SUBLANES, hd))

        q = rows8(q_ref)
        qb = q.astype(BF16)
        table_at = lambda k, h=h: table_ref[k, h]
        kn, vn = rows8(kn_ref), rows8(vn_ref)
        l_new = jnp.sum(q * kn, axis=-1, keepdims=True) * scale + table_at(0)
        logits, values = [], []
        for kk in range(MOBA_TOPK):
            sel = sel_ref[b, h, kk]
            valid = sel < nbp
            selp = jnp.minimum(sel, nbp - 1)
            for p in range(ppb):
                i = hh * n_buf + kk * ppb + p
                kp = kbuf[slot, i].astype(BF16)
                s = lax.dot_general(qb, kp, nt, preferred_element_type=F32) * scale
                pos = selp * MOBA_BLOCK + p * PAGE_SIZE + lax.broadcasted_iota(jnp.int32, s.shape, 1)
                s = s + _bias_from_distance(t_new - pos, table_at)
                logits.append(jnp.where(valid, s, NEG))
                values.append(vbuf[slot, i].astype(BF16))
        m = l_new
        for s in logits:
            m = jnp.maximum(m, jnp.max(s, axis=-1, keepdims=True))
        p_new = jnp.exp(l_new - m)
        den = p_new
        acc = p_new * vn
        for s, v in zip(logits, values):
            p = jnp.exp(s - m)
            den = den + jnp.sum(p, axis=-1, keepdims=True)
            acc = acc + jnp.dot(p.astype(BF16), v, preferred_element_type=F32)
        o_ref[hh] = acc / den


def moba_decode_attn(sel, page_table, rel_bias, q_rows, k_new_rows, v_new_rows, cache_k, cache_v, *, n_seq):
    R, HD = q_rows.shape
    hd = HD // N_HEADS
    n_pages = page_table.shape[1]
    ppb = MOBA_BLOCK // PAGE_SIZE
    nbp = n_pages // ppb
    hg = DEC_HEADS_PER_STEP
    assert n_pages % ppb == 0, "past length must be a whole number of MoBA blocks"
    assert N_HEADS % hg == 0
    row_spec = pl.BlockSpec((R, HD), lambda s, sel, pt: (0, 0))
    n_buf = hg * MOBA_TOPK * ppb
    body = functools.partial(_decode_attn_body, nbp=nbp, ppb=ppb, t_new=n_pages * PAGE_SIZE, scale=hd ** -0.5)
    return pl.pallas_call(
        body,
        grid_spec=pltpu.PrefetchScalarGridSpec(
            num_scalar_prefetch=2, grid=(n_seq * N_HEADS // hg,),
            in_specs=[pl.BlockSpec(memory_space=pltpu.SMEM), row_spec, row_spec, row_spec,
                      pl.BlockSpec(memory_space=pl.ANY), pl.BlockSpec(memory_space=pl.ANY)],
            out_specs=pl.BlockSpec((hg, V7X_SUBLANES, hd), lambda s, sel, pt: (s, 0, 0)),
            scratch_shapes=[pltpu.VMEM((2, n_buf, PAGE_SIZE, hd), F32), pltpu.VMEM((2, n_buf, PAGE_SIZE, hd), F32),
                            pltpu.SemaphoreType.DMA((2, 2, n_buf))],
        ),
        out_shape=jax.ShapeDtypeStruct((n_seq * N_HEADS, V7X_SUBLANES, hd), F32), name="moba_decode_attn",
        compiler_params=pltpu.CompilerParams(dimension_semantics=("arbitrary",)),
    )(sel, page_table, rel_bias, q_rows, k_new_rows, v_new_rows, cache_k, cache_v)


ROW_TILE = 1024
DEC_ROWS = 16
CONV_TIME_TILE = 256


def kernel(x_prompt, x_sample, state_conv, cache_k, cache_v, page_table, norm_mix_g, norm_ffn_g, a_w_in, a_b_in, a_w_dw, a_b_dw, a_ln_g, a_ln_b, a_w_out, a_b_out, kv_norm_g, w_k, w_v, b_w_q, b_w_o, rel_bias, f_w_gate, f_w_up, f_w_down, final_norm_g):
    B, S, D = x_prompt.shape
    DB, dec_seq, _ = x_sample.shape
    assert dec_seq == 1, "decode path handles one new token per sequence"
    n_a = a_w_in.shape[0]
    depth = norm_mix_g.shape[0]
    F = f_w_gate.shape[-1]
    hd = D // N_HEADS
    ppb = MOBA_BLOCK // PAGE_SIZE
    n_pages = page_table.shape[1]

    bf = lambda w: w.astype(BF16)
    a_w_in_b, a_w_out_b = bf(a_w_in), bf(a_w_out)
    w_k_b, w_v_b, w_q_b, w_o_b = bf(w_k)[None], bf(w_v)[None], bf(b_w_q), bf(b_w_o)
    w_gate_b, w_up_b, w_down_b = bf(f_w_gate), bf(f_w_up), bf(f_w_down)

    xp = x_prompt.reshape(B * S, D)
    xs = jnp.pad(x_sample.reshape(DB, D), ((0, DEC_ROWS - DB), (0, 0)))

    def mm(x, x_side, ws, n_cols, *, residual=None, outs=((F32, None),), side_dtypes=(F32,), out_scale=None,
           side_scale=None, **kw):
        res_main, res_side = residual if residual is not None else (None, None)
        return fused_matmul(x, ws, n_cols, residual=res_main, outs=outs, out_scale=out_scale,
                            side=(x_side, res_side, side_dtypes, side_scale), tm=ROW_TILE, **kw)

    conv_p, conv_s = [], []
    bias_tables = relbias_tables(rel_bias)
    page_ids = page_table.reshape(-1)
    ids_per_layer = -(-page_ids.shape[0] // n_a)
    page_sum_parts = []
    for l in range(depth):
        if l < n_a:
            b_in = a_b_in[l].reshape(1, 2 * D)
            u, u_s = mm(xp, xs, [(a_w_in_b, l, 0), (a_w_in_b, l, D)], D, gain=norm_mix_g[l],
                        biases=[(b_in, 0), (b_in, D)], epilogue="glu", name="glu")
            u3 = u.reshape(B, S, D)
            left = jnp.zeros((B, CONV_WIDTH - 1, D), F32)
            c = conv_ln_silu(u3, left, a_w_dw[l], a_b_dw[l], a_ln_g[l], a_ln_b[l],
                             tt=CONV_TIME_TILE).reshape(B * S, D)
            conv_p.append(u3[:, S - (CONV_WIDTH - 1):, :])
            c_s = conv_step(state_conv[l], u_s[:DB], a_w_dw[l], a_b_dw[l], a_ln_g[l], a_ln_b[l])
            c_s = jnp.pad(c_s, ((0, DEC_ROWS - DB), (0, 0)))
            conv_s.append(jnp.concatenate([state_conv[l][:, 1:, :], u_s[:DB, None, :]], axis=1))
            xp, xs = mm(c, c_s, [(a_w_out_b, l, 0)], D, biases=[(a_b_out[l].reshape(1, D), 0)],
                        residual=(xp, xs), name="convout")
        else:
            j = l - n_a
            if l == n_a:
                kv_outs = ((F32, hd), (BF16, None))
                k_heads, k_b, k_s = mm(xp, xs, [(w_k_b, 0, 0)], D, gain=kv_norm_g, outs=kv_outs, name="kproj")
                v_heads, v_b, v_s = mm(xp, xs, [(w_v_b, 0, 0)], D, gain=kv_norm_g, outs=kv_outs, name="vproj")
                page_sums = jnp.concatenate(page_sum_parts).reshape(DB, n_pages // ppb, ppb * D)
            q, q_s = mm(xp, xs, [(w_q_b, j, 0)], D, gain=norm_mix_g[l], outs=((BF16, None),),
                        out_scale=hd ** -0.5 * LOG2E, name="qproj")
            o = moba_prompt_attn(q, k_b, v_b, bias_tables, n_seq=B, seq=S)
            picks = moba_gate_topk(q_s, page_sums, k_s, n_seq=DB, ppb=ppb)
            sel = picks[:, :MOBA_TOPK, :N_HEADS].transpose(0, 2, 1)
            o8 = moba_decode_attn(sel, page_table, rel_bias, q_s, k_s, v_s, cache_k, cache_v, n_seq=DB)
            o_s = jnp.pad(o8[:, 0, :].reshape(DB, D), ((0, DEC_ROWS - DB), (0, 0))).astype(BF16)
            xp, xs = mm(o, o_s, [(w_o_b, j, 0)], D, residual=(xp, xs), name="oproj")
        ids = page_ids[l * ids_per_layer:(l + 1) * ids_per_layer] if l < n_a else None
        up = mm(xp, xs, [(w_gate_b, l, 0), (w_up_b, l, 0)], F, gain=norm_ffn_g[l], epilogue="swiglu",
                outs=((BF16, None),), side_dtypes=(BF16,),
                page_stream=None if ids is None else (cache_k, ids), name="ffn_up")
        act, act_s = up[0], up[1]
        if ids is not None:
            page_sum_parts.append(up[2][:ids.shape[0]])
        xp, xs = mm(act, act_s, [(w_down_b, l, 0)], D, residual=(xp, xs), name="ffn_down")

    y_prompt = rmsnorm_rows(xp, final_norm_g, tm=ROW_TILE // 2).reshape(B, S, D)
    y_sample = rmsnorm_rows(xs, final_norm_g, tm=DEC_ROWS)[:DB].reshape(DB, 1, D)
    return (y_prompt, y_sample, jnp.stack(conv_p), jnp.stack(conv_s),
            k_heads.reshape(B, S, N_HEADS, hd), v_heads.reshape(B, S, N_HEADS, hd),
            k_s[:DB].reshape(DB, 1, N_HEADS, hd), v_s[:DB].reshape(DB, 1, N_HEADS, hd))
```

```python
import functools
import math

import numpy as np
import jax
import jax.numpy as jnp
from jax import lax
from jax.experimental import pallas as pl
from jax.experimental.pallas import tpu as pltpu

F32 = jnp.float32
BF16 = jnp.bfloat16

N_HEADS = 16
CONV_WIDTH = 31
MOBA_BLOCK = 256
MOBA_TOPK = 3
NUM_BUCKETS = 32
MAX_DISTANCE = 128
PAGE_SIZE = 128
EPS = 1e-6
NEG = -1e30
LOG2E = math.log2(math.e)

V7X_VMEM_BYTES = 64 * 1024 * 1024
V7X_LANES = 128
V7X_SUBLANES = 8
VMEM_CAP = V7X_VMEM_BYTES - 4 * 1024 * 1024


def _vmem_limit(est_bytes):
    return int(min(max(est_bytes + (8 << 20), 16 << 20), VMEM_CAP))


def _bucket_thresholds():
    max_exact = NUM_BUCKETS // 2
    n = np.arange(0, 4 * MAX_DISTANCE, dtype=np.int64)
    nf = np.maximum(n, max_exact).astype(np.float32)
    large = max_exact + (np.log(nf / np.float32(max_exact)) / np.float32(math.log(MAX_DISTANCE / max_exact))
                         * np.float32(NUM_BUCKETS - max_exact)).astype(np.int32)
    large = np.minimum(large, NUM_BUCKETS - 1)
    bucket = np.where(n < max_exact, n, large)
    assert np.all(np.diff(bucket) >= 0) and bucket[-1] == NUM_BUCKETS - 1
    return [int(np.argmax(bucket >= k)) for k in range(1, NUM_BUCKETS)]


_BUCKET_START = _bucket_thresholds()


def _bias_from_distance(dist, table_at):
    bias = jnp.full(dist.shape, table_at(0), F32)
    for k in range(1, NUM_BUCKETS):
        bias = jnp.where(dist >= _BUCKET_START[k - 1], table_at(k), bias)
    return bias


def _mm_body(*refs, has_norm, n_w, has_bias, has_res, has_side, epilogue, out_scales, n_outs, n_pages, n_chunks):
    it = iter(refs)
    if n_pages:
        next(it)
    x_refs = [next(it) for _ in range(1 + has_side)]
    g_ref = next(it) if has_norm else None
    w_refs = [next(it) for _ in range(n_w)]
    b_refs = [next(it) for _ in range(n_w)] if has_bias else []
    res_refs = [next(it) for _ in range(1 + has_side)] if has_res else [None, None]
    page_refs = [next(it) for _ in range(n_pages)]
    out_refs = [[next(it) for _ in range(n)] for n in n_outs]
    page_out = next(it) if n_pages else None

    def stream(xb, res_ref, outs, out_scale):
        accs = [jnp.dot(xb, w[...], preferred_element_type=F32) for w in w_refs]
        if has_bias:
            accs = [a + b[...] for a, b in zip(accs, b_refs)]
        if epilogue == "glu":
            y = accs[0] * jax.nn.sigmoid(accs[1])
        elif epilogue == "swiglu":
            y = jax.nn.silu(accs[0]) * accs[1]
        else:
            y = accs[0]
        if out_scale is not None:
            y = y * out_scale
        if has_res:
            y = res_ref[...] + y
        for o in outs:
            if len(o.shape) == 3:
                hd = o.shape[-1]
                for h in range(o.shape[1]):
                    o[:, h, :] = y[:, h * hd:(h + 1) * hd].astype(o.dtype)
            else:
                o[...] = y.astype(o.dtype)

    if has_norm:
        xn_ref = next(it)
        side_xn_ref = next(it) if has_side else None
        i, j = pl.program_id(0), pl.program_id(1)
        chunk_rows = x_refs[0].shape[0]

        def normed(x_ref):
            x = x_ref[...]
            ms = jnp.mean(x * x, axis=-1, keepdims=True)
            return (x * lax.rsqrt(ms + EPS) * g_ref[...]).astype(BF16)

        def norm_chunk():
            c = jnp.minimum(j, n_chunks - 1)
            rows = pl.ds(pl.multiple_of(c * chunk_rows, chunk_rows), chunk_rows)
            xn_ref[i % 2, rows, :] = normed(x_refs[0])

        @pl.when(i == 0)
        def _():
            norm_chunk()
            if has_side:
                side_xn_ref[...] = normed(x_refs[1])

        @pl.when(i > 0)
        def _():
            norm_chunk()
            stream(xn_ref[(i - 1) % 2], res_refs[0], out_refs[0], out_scales[0])

        if has_side:
            @pl.when(i == 1)
            def _():
                stream(side_xn_ref[...], res_refs[1], out_refs[1], out_scales[1])
    else:
        stream(x_refs[0][...], res_refs[0], out_refs[0], out_scales[0])
        if has_side:
            @pl.when(pl.program_id(0) == 0)
            def _():
                stream(x_refs[1][...], res_refs[1], out_refs[1], out_scales[1])

    for k, page_ref in enumerate(page_refs):
        page_out[k] = jnp.sum(page_ref[...], axis=0)


MATMUL_VMEM_BUDGET = 48 * 1024 * 1024
MAX_NORM_CHUNKS = 8


def fused_matmul(x, ws, n_cols, *, gain=None, biases=None, residual=None, epilogue="none",
                 outs=((F32, None),), out_scale=None, side=None, page_stream=None, tm, name):
    M, K = x.shape
    has_norm = gain is not None
    has_side = side is not None
    ni = M // tm
    assert M % tm == 0
    n_rows = ni + 1 if has_norm else ni
    out_bytes = sum(jnp.dtype(dt).itemsize for dt, _ in outs) + (4 if residual is not None else 0)

    def chunks(tn):
        nc = min(n_cols // tn, MAX_NORM_CHUNKS)
        while tm % (nc * 2 * V7X_SUBLANES):
            nc -= 1
        return nc

    def pages_per_step(tn):
        return -(-page_stream[1].shape[0] // (n_rows * (n_cols // tn))) if page_stream is not None else 0

    def estimate(tn):
        page_bytes = 2 * pages_per_step(tn) * math.prod(page_stream[0].shape[1:]) * 4 if page_stream else 0
        x_bytes = 2 * (tm // chunks(tn)) * K * 4 + 2 * tm * K * 2 if has_norm else 2 * tm * K * x.dtype.itemsize
        return (x_bytes + 2 * len(ws) * K * tn * 2 + 2 * tm * tn * out_bytes + (len(ws) + 1) * tm * tn * 4
                + page_bytes)

    def legal(tn):
        heads_ok = all(hd is None or (tn // hd) % V7X_SUBLANES == 0 or tn == n_cols for _, hd in outs)
        offs_ok = all(off % tn == 0 for _, _, off in ws)
        return n_cols % tn == 0 and heads_ok and offs_ok

    tiles = [t for t in (2048, 1024, 512, 256, 128) if legal(t)]
    fitting = [t for t in tiles if estimate(t) <= MATMUL_VMEM_BUDGET]
    tn = fitting[0] if fitting else tiles[-1]
    nj = n_cols // tn
    nc = chunks(tn)
    grid = (n_rows, nj)
    if has_norm:
        tile = lambda i: jnp.maximum(i - 1, 0)
        col = lambda i, j: jnp.where(i == 0, 0, j)
        side_col = lambda i, j: jnp.where(i == 0, 0, jnp.where(i == 1, j, nj - 1))
        in_specs = [pl.BlockSpec((tm // nc, K),
                                 lambda i, j, *_: (jnp.minimum(i, ni - 1) * nc + jnp.minimum(j, nc - 1), 0))]
    else:
        tile = lambda i: i
        col = lambda i, j: j
        side_col = lambda i, j: jnp.where(i == 0, j, nj - 1)
        in_specs = [pl.BlockSpec((tm, K), lambda i, j, *_: (i, 0))]
    args = [x]
    if has_side:
        x_side, res_side, side_dtypes, side_scale = side
        R = x_side.shape[0]
        in_specs.append(pl.BlockSpec((R, K), lambda i, j, *_: (0, 0)))
        args.append(x_side)
    if has_norm:
        in_specs.append(pl.BlockSpec((1, K), lambda i, j, *_: (0, 0)))
        args.append(gain.reshape(1, K))
    for w, layer, off in ws:
        in_specs.append(pl.BlockSpec((None, K, tn),
                                     lambda i, j, *_, layer=layer, ob=off // tn: (layer, 0, col(i, j) + ob)))
        args.append(w)
    if biases is not None:
        for b, off in biases:
            in_specs.append(pl.BlockSpec((1, tn), lambda i, j, *_, ob=off // tn: (0, col(i, j) + ob)))
            args.append(b)
    if residual is not None:
        in_specs.append(pl.BlockSpec((tm, tn), lambda i, j, *_: (tile(i), col(i, j))))
        args.append(residual)
        if has_side:
            in_specs.append(pl.BlockSpec((R, tn), lambda i, j, *_: (0, side_col(i, j))))
            args.append(res_side)
    out_specs, out_shape = [], []
    for dt, hd in outs:
        if hd is None:
            out_specs.append(pl.BlockSpec((tm, tn), lambda i, j, *_: (tile(i), col(i, j))))
            out_shape.append(jax.ShapeDtypeStruct((M, n_cols), dt))
        else:
            out_specs.append(pl.BlockSpec((tm, tn // hd, hd), lambda i, j, *_: (tile(i), col(i, j), 0)))
            out_shape.append(jax.ShapeDtypeStruct((M, n_cols // hd, hd), dt))
    n_outs, out_scales = [len(outs)], [out_scale]
    if has_side:
        for dt in side_dtypes:
            out_specs.append(pl.BlockSpec((R, tn), lambda i, j, *_: (0, side_col(i, j))))
            out_shape.append(jax.ShapeDtypeStruct((R, n_cols), dt))
        n_outs.append(len(side_dtypes))
        out_scales.append(side_scale)
    prefetch = []
    per_step = pages_per_step(tn)
    if page_stream is not None:
        cache, ids = page_stream
        n_slots = n_rows * nj * per_step
        assert ids.shape[0] <= n_slots
        prefetch = [jnp.concatenate([ids, jnp.broadcast_to(ids[-1:], (n_slots - ids.shape[0],))])]
        for k in range(per_step):
            in_specs.append(pl.BlockSpec(
                (None,) + cache.shape[1:],
                lambda i, j, ids_ref, k=k: (ids_ref[(i * nj + j) * per_step + k], 0, 0, 0)))
            args.append(cache)
        out_specs.append(pl.BlockSpec((per_step,) + cache.shape[2:], lambda i, j, *_: (i * nj + j, 0, 0)))
        out_shape.append(jax.ShapeDtypeStruct((n_slots,) + cache.shape[2:], F32))
    scratch = []
    if has_norm:
        scratch = [pltpu.VMEM((2, tm, K), BF16)] + ([pltpu.VMEM((R, K), BF16)] if has_side else [])
    body = functools.partial(_mm_body, has_norm=has_norm, n_w=len(ws), has_bias=biases is not None,
                             has_res=residual is not None, has_side=has_side, epilogue=epilogue,
                             out_scales=out_scales, n_outs=n_outs, n_pages=per_step, n_chunks=nc)
    res = pl.pallas_call(
        body,
        grid_spec=pltpu.PrefetchScalarGridSpec(num_scalar_prefetch=len(prefetch), grid=grid, in_specs=in_specs,
                                               out_specs=out_specs, scratch_shapes=scratch),
        out_shape=out_shape, name=name,
        compiler_params=pltpu.CompilerParams(dimension_semantics=("arbitrary", "arbitrary"),
                                             vmem_limit_bytes=_vmem_limit(estimate(tn))),
    )(*prefetch, *args)
    return res if len(res) > 1 else res[0]


def _rmsnorm_body(x_ref, g_ref, o_ref):
    x = x_ref[...]
    ms = jnp.mean(x * x, axis=-1, keepdims=True)
    o_ref[...] = x * lax.rsqrt(ms + EPS) * g_ref[...]


def rmsnorm_rows(x, gain, *, tm):
    M, D = x.shape
    return pl.pallas_call(
        _rmsnorm_body, grid=(M // tm,),
        in_specs=[pl.BlockSpec((tm, D), lambda i: (i, 0)), pl.BlockSpec((1, D), lambda i: (0, 0))],
        out_specs=pl.BlockSpec((tm, D), lambda i: (i, 0)),
        out_shape=jax.ShapeDtypeStruct((M, D), F32), name="rmsnorm_rows",
        compiler_params=pltpu.CompilerParams(dimension_semantics=("arbitrary",),
                                             vmem_limit_bytes=_vmem_limit(4 * tm * D * 4)),
    )(x, gain.reshape(1, D))


CONV_HALO = 32
CONV_ROWS = 64


def _ln_silu(c, g, b):
    mu = jnp.mean(c, axis=-1, keepdims=True)
    xc = c - mu
    var = jnp.mean(xc * xc, axis=-1, keepdims=True)
    y = xc * lax.rsqrt(var + EPS) * g + b
    return y * jax.nn.sigmoid(y)


def _conv_body(u_ref, halo_ref, left_ref, w_ref, bdw_ref, g_ref, b_ref, o_ref, ext_ref, y_ref, *, tt):
    first = pl.program_id(1) == 0
    ext_ref[0:CONV_HALO, :] = jnp.where(first, left_ref[...], halo_ref[...])
    ext_ref[CONV_HALO:, :] = u_ref[...]
    d_model = u_ref.shape[-1]
    lead = CONV_HALO - (CONV_WIDTH - 1)

    def chunk(c, carry):
        lanes = pl.ds(pl.multiple_of(c * V7X_LANES, V7X_LANES), V7X_LANES)
        w = w_ref[:, lanes]
        for r0 in range(0, tt, CONV_ROWS):
            n_win = CONV_ROWS + CONV_HALO
            win = ext_ref[r0:r0 + n_win, lanes]
            acc = jnp.zeros((CONV_ROWS, V7X_LANES), F32)
            for res in range(V7X_SUBLANES):
                shifted = win if res == 0 else pltpu.roll(win, n_win - res, axis=0)
                for off in range(res, lead + CONV_WIDTH, V7X_SUBLANES):
                    k = off - lead
                    if k >= 0:
                        acc = acc + shifted[off - res:off - res + CONV_ROWS, :] * w[k:k + 1, :]
            y_ref[r0:r0 + CONV_ROWS, lanes] = acc
        return carry

    lax.fori_loop(0, d_model // V7X_LANES, chunk, 0)
    c = y_ref[...] + bdw_ref[...]
    o_ref[...] = _ln_silu(c, g_ref[...], b_ref[...]).astype(o_ref.dtype)


def conv_ln_silu(u, left, w_dw, b_dw, ln_g, ln_b, *, tt):
    B, T, D = u.shape
    left_pad = jnp.pad(left, ((0, 0), (CONV_HALO - (CONV_WIDTH - 1), 0), (0, 0)))
    hb = tt // CONV_HALO
    row = lambda a: a.reshape(1, D)
    est = 2 * tt * D * 4 + 4 * CONV_HALO * D * 4 + (2 * tt + CONV_HALO) * D * 4 + 2 * tt * D * 2 + 6 * tt * D * 4
    return pl.pallas_call(
        functools.partial(_conv_body, tt=tt), grid=(B, T // tt),
        in_specs=[
            pl.BlockSpec((None, tt, D), lambda b, i: (b, i, 0)),
            pl.BlockSpec((None, CONV_HALO, D), lambda b, i: (b, jnp.maximum(i * hb - 1, 0), 0)),
            pl.BlockSpec((None, CONV_HALO, D), lambda b, i: (b, 0, 0)),
            pl.BlockSpec((CONV_WIDTH, D), lambda b, i: (0, 0)),
            pl.BlockSpec((1, D), lambda b, i: (0, 0)),
            pl.BlockSpec((1, D), lambda b, i: (0, 0)),
            pl.BlockSpec((1, D), lambda b, i: (0, 0)),
        ],
        out_specs=pl.BlockSpec((None, tt, D), lambda b, i: (b, i, 0)),
        out_shape=jax.ShapeDtypeStruct((B, T, D), BF16),
        scratch_shapes=[pltpu.VMEM((tt + CONV_HALO, D), F32), pltpu.VMEM((tt, D), F32)],
        name="conv_ln_silu",
        compiler_params=pltpu.CompilerParams(dimension_semantics=("arbitrary", "arbitrary"),
                                             vmem_limit_bytes=_vmem_limit(est)),
    )(u, u, left_pad, w_dw, row(b_dw), row(ln_g), row(ln_b))


def _conv_step_body(state_ref, u_ref, w_ref, bdw_ref, g_ref, b_ref, o_ref):
    acc = u_ref[...] * w_ref[CONV_WIDTH - 1:CONV_WIDTH, :]
    for k in range(CONV_WIDTH - 1):
        acc = acc + state_ref[:, k, :] * w_ref[k:k + 1, :]
    c = acc + bdw_ref[...]
    o_ref[...] = _ln_silu(c, g_ref[...], b_ref[...]).astype(o_ref.dtype)


def conv_step(state, u_new, w_dw, b_dw, ln_g, ln_b):
    NB, _, D = state.shape
    row = lambda a: a.reshape(1, D)
    return pl.pallas_call(
        _conv_step_body,
        out_shape=jax.ShapeDtypeStruct((NB, D), BF16), name="conv_step",
    )(state, u_new, w_dw, row(b_dw), row(ln_g), row(ln_b))


def _relbias_body(table_ref, o_ref):
    h = pl.program_id(0)
    a = lax.broadcasted_iota(jnp.int32, (MOBA_BLOCK, MOBA_BLOCK), 0)
    b = lax.broadcasted_iota(jnp.int32, (MOBA_BLOCK, MOBA_BLOCK), 1)
    far = table_ref[NUM_BUCKETS - 1, h]
    table_at = lambda k: table_ref[k, h]
    d_own = a - b
    own = (_bias_from_distance(d_own, table_at) - far) * LOG2E
    o_ref[0] = jnp.where(d_own >= 0, own, NEG)
    o_ref[1] = (_bias_from_distance(d_own + MOBA_BLOCK, table_at) - far) * LOG2E


def relbias_tables(rel_bias):
    return pl.pallas_call(
        _relbias_body, grid=(N_HEADS,),
        in_specs=[pl.BlockSpec(memory_space=pltpu.SMEM)],
        out_specs=pl.BlockSpec((None, 2, MOBA_BLOCK, MOBA_BLOCK), lambda h: (h, 0, 0, 0)),
        out_shape=jax.ShapeDtypeStruct((N_HEADS, 2, MOBA_BLOCK, MOBA_BLOCK), F32), name="relbias_tables",
    )(rel_bias)


GATE_ROWS = 16


def _moba_prompt_body(q_ref, k_ref, v_ref, bias_ref, o_ref, kx_ref, *, nb):
    blk = MOBA_BLOCK
    hd = q_ref.shape[-1]
    seq = nb * blk
    nt = (((1,), (1,)), ((), ()))

    k = k_ref[...]
    kx_ref[:, 0:hd] = k
    kr = lax.broadcasted_iota(jnp.int32, (seq, V7X_LANES), 0)
    kc = lax.broadcasted_iota(jnp.int32, (seq, V7X_LANES), 1)
    kx_ref[:, hd:] = (kr // blk == kc).astype(BF16)
    r = lax.broadcasted_iota(jnp.int32, (GATE_ROWS, seq), 0)
    c = lax.broadcasted_iota(jnp.int32, (GATE_ROWS, seq), 1)
    avg = jnp.where(c // blk == r, 1.0 / blk, 0.0).astype(BF16)
    km = jnp.dot(avg, k, preferred_element_type=F32)
    km_hi = km.astype(BF16)
    km_lo = (km - km_hi.astype(F32)).astype(BF16)

    q = q_ref[...]
    g2 = (lax.dot_general(km_hi, q, nt, preferred_element_type=F32)
          + lax.dot_general(km_lo, q, nt, preferred_element_type=F32))
    gate = g2[0:V7X_SUBLANES]
    row = lax.broadcasted_iota(jnp.int32, gate.shape, 0)
    own_blk = lax.broadcasted_iota(jnp.int32, gate.shape, 1) // blk
    gate = jnp.where(row < own_blk, gate, NEG)
    beaten = jnp.zeros(gate.shape, jnp.int32)
    for jp in range(nb):
        gj = gate[jp:jp + 1, :]
        beaten = beaten + ((gj > gate) | ((gj == gate) & (jp < row))).astype(jnp.int32)
    chosen = (beaten < MOBA_TOPK) & (row < own_blk)
    mask_t = jnp.where(chosen | (row == own_blk), 0.0, NEG)
    mask_t = jnp.concatenate([mask_t, jnp.zeros((V7X_LANES - V7X_SUBLANES, seq), F32)], axis=0)
    qx = jnp.concatenate([q, mask_t.T.astype(BF16)], axis=1)

    for own in range(nb):
        n = (own + 1) * blk
        s = lax.dot_general(qx[own * blk:n, :], kx_ref[0:n, :], nt, preferred_element_type=F32)
        parts = [s[:, n - blk:] + bias_ref[0]]
        if own >= 1:
            parts.insert(0, s[:, n - 2 * blk:n - blk] + bias_ref[1])
        if own >= 2:
            parts.insert(0, s[:, :n - 2 * blk])
        s = jnp.concatenate(parts, axis=1) if len(parts) > 1 else parts[0]
        m = jnp.max(s, axis=-1, keepdims=True)
        p = jnp.exp2(s - m)
        l = jnp.sum(p, axis=-1, keepdims=True)
        pv = jnp.dot(p.astype(BF16), v_ref[0:n, :], preferred_element_type=F32)
        o_ref[own * blk:n, :] = (pv * (1.0 / l)).astype(o_ref.dtype)


def moba_prompt_attn(q, k, v, bias, *, n_seq, seq):
    M, HD = q.shape
    hd = HD // N_HEADS
    nb = seq // MOBA_BLOCK
    assert seq % MOBA_BLOCK == 0 and nb <= V7X_SUBLANES
    head_spec = pl.BlockSpec((seq, hd), lambda b, h: (b, h))
    return pl.pallas_call(
        functools.partial(_moba_prompt_body, nb=nb), grid=(n_seq, N_HEADS),
        in_specs=[head_spec, head_spec, head_spec,
                  pl.BlockSpec((None, 2, MOBA_BLOCK, MOBA_BLOCK), lambda b, h: (h, 0, 0, 0))],
        out_specs=head_spec,
        out_shape=jax.ShapeDtypeStruct((M, HD), BF16),
        scratch_shapes=[pltpu.VMEM((seq, hd + V7X_LANES), BF16)],
        name="moba_prompt_attn",
        compiler_params=pltpu.CompilerParams(dimension_semantics=("arbitrary", "arbitrary"),
                                             vmem_limit_bytes=_vmem_limit(8 * MOBA_BLOCK * seq * 4)),
    )(q, k, v, bias)


def _gate_topk_body(q_ref, sums_ref, knew_ref, o_ref, *, nbp, ppb, own, inv_rows):
    b = pl.program_id(0)
    HD = q_ref.shape[-1]
    hd = HD // N_HEADS
    q = q_ref[pl.ds(b, 1), :]
    r = lax.broadcasted_iota(jnp.int32, (HD, V7X_LANES), 0)
    c = lax.broadcasted_iota(jnp.int32, (HD, V7X_LANES), 1)
    ind = (r // hd == c).astype(BF16)

    def head_sums(prod):
        hi = prod.astype(BF16)
        mid = (prod - hi.astype(F32)).astype(BF16)
        lo = (prod - hi.astype(F32) - mid.astype(F32)).astype(BF16)
        return (jnp.dot(hi, ind, preferred_element_type=F32) + jnp.dot(mid, ind, preferred_element_type=F32)
                + jnp.dot(lo, ind, preferred_element_type=F32))

    block_sums = sums_ref[:, 0:HD]
    for p in range(1, ppb):
        block_sums = block_sums + sums_ref[:, p * HD:(p + 1) * HD]
    means = block_sums * inv_rows
    gate_past = head_sums(means * q)
    k_new = jnp.broadcast_to(knew_ref[pl.ds(b, 1), :], (V7X_SUBLANES, HD))
    gate_rec = head_sums((k_new * inv_rows) * q)
    gate = jnp.concatenate([gate_past, gate_rec], axis=0)
    idx = lax.broadcasted_iota(jnp.int32, gate.shape, 0)
    gate = jnp.where(idx < own, gate, NEG)
    gate = jnp.where(idx <= nbp, gate, -jnp.inf)
    picks = []
    for _ in range(MOBA_TOPK):
        mx = jnp.max(gate, axis=0, keepdims=True)
        pick = jnp.min(jnp.where(gate == mx, idx, nbp + V7X_SUBLANES), axis=0, keepdims=True)
        picks.append(pick)
        gate = jnp.where(idx == pick, -jnp.inf, gate)
    picks.append(jnp.zeros((V7X_SUBLANES - MOBA_TOPK, V7X_LANES), jnp.int32))
    o_ref[...] = jnp.concatenate(picks, axis=0)


def moba_gate_topk(q_rows, page_sums, k_new_rows, *, n_seq, ppb):
    R, HD = q_rows.shape
    _, nbp, _ = page_sums.shape
    return pl.pallas_call(
        functools.partial(_gate_topk_body, nbp=nbp, ppb=ppb, own=nbp, inv_rows=1.0 / MOBA_BLOCK), grid=(n_seq,),
        in_specs=[
            pl.BlockSpec((R, HD), lambda b: (0, 0)),
            pl.BlockSpec((None, nbp, ppb * HD), lambda b: (b, 0, 0)),
            pl.BlockSpec((R, HD), lambda b: (0, 0)),
        ],
        out_specs=pl.BlockSpec((None, V7X_SUBLANES, V7X_LANES), lambda b: (b, 0, 0)),
        out_shape=jax.ShapeDtypeStruct((n_seq, V7X_SUBLANES, V7X_LANES), jnp.int32), name="moba_gate_topk",
        compiler_params=pltpu.CompilerParams(dimension_semantics=("arbitrary",)),
    )(q_rows, page_sums, k_new_rows)


DEC_HEADS_PER_STEP = 4


def _decode_attn_body(sel_ref, pt_ref, table_ref, q_ref, kn_ref, vn_ref, ck_ref, cv_ref, o_ref,
                      kbuf, vbuf, sem, *, nbp, ppb, t_new, scale):
    step = pl.program_id(0)
    hd = kbuf.shape[-1]
    hg = DEC_HEADS_PER_STEP
    groups = N_HEADS // hg
    n_buf = MOBA_TOPK * ppb

    def page_copies(st, slot):
        b = st // groups
        out = []
        for hh in range(hg):
            h = (st % groups) * hg + hh
            for kk in range(MOBA_TOPK):
                blk = jnp.minimum(sel_ref[b, h, kk], nbp - 1)
                for p in range(ppb):
                    page = pt_ref[b, blk * ppb + p]
                    i = hh * n_buf + kk * ppb + p
                    out.append(pltpu.make_async_copy(ck_ref.at[page, :, h, :], kbuf.at[slot, i], sem.at[0, slot, i]))
                    out.append(pltpu.make_async_copy(cv_ref.at[page, :, h, :], vbuf.at[slot, i], sem.at[1, slot, i]))
        return out

    slot = step % 2

    @pl.when(step == 0)
    def _():
        for c in page_copies(step, slot):
            c.start()

    @pl.when(step + 1 < pl.num_programs(0))
    def _():
        for c in page_copies(step + 1, 1 - slot):
            c.start()

    for c in page_copies(step, slot):
        c.wait()

    b = step // groups
    nt = (((1,), (1,)), ((), ()))
    for hh in range(hg):
        h = (step % groups) * hg + hh
        lanes = pl.ds(pl.multiple_of(h * hd, hd), hd)

        def rows8(ref):
            x = ref[:, lanes]
            r = lax.broadcasted_iota(jnp.int32, x.shape, 0)
            row = jnp.sum(jnp.where(r == b, x, 0.0), axis=0, keepdims=True)
            return jnp.broadcast_to(row, (V7X_SUBLANES, hd))

        q = rows8(q_ref)
        qb = q.astype(BF16)
        table_at = lambda k, h=h: table_ref[k, h]
        kn, vn = rows8(kn_ref), rows8(vn_ref)
        l_new = jnp.sum(q * kn, axis=-1, keepdims=True) * scale + table_at(0)
        logits, values = [], []
        for kk in range(MOBA_TOPK):
            sel = sel_ref[b, h, kk]
            valid = sel < nbp
            selp = jnp.minimum(sel, nbp - 1)
            for p in range(ppb):
                i = hh * n_buf + kk * ppb + p
                kp = kbuf[slot, i].astype(BF16)
                s = lax.dot_general(qb, kp, nt, preferred_element_type=F32) * scale
                pos = selp * MOBA_BLOCK + p * PAGE_SIZE + lax.broadcasted_iota(jnp.int32, s.shape, 1)
                s = s + _bias_from_distance(t_new - pos, table_at)
                logits.append(jnp.where(valid, s, NEG))
                values.append(vbuf[slot, i].astype(BF16))
        m = l_new
        for s in logits:
            m = jnp.maximum(m, jnp.max(s, axis=-1, keepdims=True))
        p_new = jnp.exp(l_new - m)
        den = p_new
        acc = p_new * vn
        for s, v in zip(logits, values):
            p = jnp.exp(s - m)
            den = den + jnp.sum(p, axis=-1, keepdims=True)
            acc = acc + jnp.dot(p.astype(BF16), v, preferred_element_type=F32)
        o_ref[hh] = acc / den


def moba_decode_attn(sel, page_table, rel_bias, q_rows, k_new_rows, v_new_rows, cache_k, cache_v, *, n_seq):
    R, HD = q_rows.shape
    hd = HD // N_HEADS
    n_pages = page_table.shape[1]
    ppb = MOBA_BLOCK // PAGE_SIZE
    nbp = n_pages // ppb
    hg = DEC_HEADS_PER_STEP
    assert n_pages % ppb == 0, "past length must be a whole number of MoBA blocks"
    assert N_HEADS % hg == 0
    row_spec = pl.BlockSpec((R, HD), lambda s, sel, pt: (0, 0))
    n_buf = hg * MOBA_TOPK * ppb
    body = functools.partial(_decode_attn_body, nbp=nbp, ppb=ppb, t_new=n_pages * PAGE_SIZE, scale=hd ** -0.5)
    return pl.pallas_call(
        body,
        grid_spec=pltpu.PrefetchScalarGridSpec(
            num_scalar_prefetch=2, grid=(n_seq * N_HEADS // hg,),
            in_specs=[pl.BlockSpec(memory_space=pltpu.SMEM), row_spec, row_spec, row_spec,
                      pl.BlockSpec(memory_space=pl.ANY), pl.BlockSpec(memory_space=pl.ANY)],
            out_specs=pl.BlockSpec((hg, V7X_SUBLANES, hd), lambda s, sel, pt: (s, 0, 0)),
            scratch_shapes=[pltpu.VMEM((2, n_buf, PAGE_SIZE, hd), F32), pltpu.VMEM((2, n_buf, PAGE_SIZE, hd), F32),
                            pltpu.SemaphoreType.DMA((2, 2, n_buf))],
        ),
        out_shape=jax.ShapeDtypeStruct((n_seq * N_HEADS, V7X_SUBLANES, hd), F32), name="moba_decode_attn",
        compiler_params=pltpu.CompilerParams(dimension_semantics=("arbitrary",)),
    )(sel, page_table, rel_bias, q_rows, k_new_rows, v_new_rows, cache_k, cache_v)


ROW_TILE = 1024
DEC_ROWS = 16
CONV_TIME_TILE = 256


def kernel(x_prompt, x_sample, state_conv, cache_k, cache_v, page_table, norm_mix_g, norm_ffn_g, a_w_in, a_b_in, a_w_dw, a_b_dw, a_ln_g, a_ln_b, a_w_out, a_b_out, kv_norm_g, w_k, w_v, b_w_q, b_w_o, rel_bias, f_w_gate, f_w_up, f_w_down, final_norm_g):
    B, S, D = x_prompt.shape
    DB, dec_seq, _ = x_sample.shape
    assert dec_seq == 1, "decode path handles one new token per sequence"
    n_a = a_w_in.shape[0]
    depth = norm_mix_g.shape[0]
    F = f_w_gate.shape[-1]
    hd = D // N_HEADS
    ppb = MOBA_BLOCK // PAGE_SIZE
    n_pages = page_table.shape[1]

    bf = lambda w: w.astype(BF16)
    a_w_in_b, a_w_out_b = bf(a_w_in), bf(a_w_out)
    w_k_b, w_v_b, w_q_b, w_o_b = bf(w_k)[None], bf(w_v)[None], bf(b_w_q), bf(b_w_o)
    w_gate_b, w_up_b, w_down_b = bf(f_w_gate), bf(f_w_up), bf(f_w_down)

    xp = x_prompt.reshape(B * S, D)
    xs = jnp.pad(x_sample.reshape(DB, D), ((0, DEC_ROWS - DB), (0, 0)))

    def mm(x, x_side, ws, n_cols, *, residual=None, outs=((F32, None),), side_dtypes=(F32,), out_scale=None,
           side_scale=None, **kw):
        res_main, res_side = residual if residual is not None else (None, None)
        return fused_matmul(x, ws, n_cols, residual=res_main, outs=outs, out_scale=out_scale,
                            side=(x_side, res_side, side_dtypes, side_scale), tm=ROW_TILE, **kw)

    conv_p, conv_s = [], []
    bias_tables = relbias_tables(rel_bias)
    page_ids = page_table.reshape(-1)
    ids_per_layer = -(-page_ids.shape[0] // n_a)
    page_sum_parts = []
    for l in range(depth):
        if l < n_a:
            b_in = a_b_in[l].reshape(1, 2 * D)
            u, u_s = mm(xp, xs, [(a_w_in_b, l, 0), (a_w_in_b, l, D)], D, gain=norm_mix_g[l],
                        biases=[(b_in, 0), (b_in, D)], epilogue="glu", name="glu")
            u3 = u.reshape(B, S, D)
            left = jnp.zeros((B, CONV_WIDTH - 1, D), F32)
            c = conv_ln_silu(u3, left, a_w_dw[l], a_b_dw[l], a_ln_g[l], a_ln_b[l],
                             tt=CONV_TIME_TILE).reshape(B * S, D)
            conv_p.append(u3[:, S - (CONV_WIDTH - 1):, :])
            c_s = conv_step(state_conv[l], u_s[:DB], a_w_dw[l], a_b_dw[l], a_ln_g[l], a_ln_b[l])
            c_s = jnp.pad(c_s, ((0, DEC_ROWS - DB), (0, 0)))
            conv_s.append(jnp.concatenate([state_conv[l][:, 1:, :], u_s[:DB, None, :]], axis=1))
            xp, xs = mm(c, c_s, [(a_w_out_b, l, 0)], D, biases=[(a_b_out[l].reshape(1, D), 0)],
                        residual=(xp, xs), name="convout")
        else:
            j = l - n_a
            if l == n_a:
                kv_outs = ((F32, hd), (BF16, None))
                k_heads, k_b, k_s = mm(xp, xs, [(w_k_b, 0, 0)], D, gain=kv_norm_g, outs=kv_outs, name="kproj")
                v_heads, v_b, v_s = mm(xp, xs, [(w_v_b, 0, 0)], D, gain=kv_norm_g, outs=kv_outs, name="vproj")
                page_sums = jnp.concatenate(page_sum_parts).reshape(DB, n_pages // ppb, ppb * D)
            q, q_s = mm(xp, xs, [(w_q_b, j, 0)], D, gain=norm_mix_g[l], outs=((BF16, None),),
                        out_scale=hd ** -0.5 * LOG2E, name="qproj")
            o = moba_prompt_attn(q, k_b, v_b, bias_tables, n_seq=B, seq=S)
            picks = moba_gate_topk(q_s, page_sums, k_s, n_seq=DB, ppb=ppb)
            sel = picks[:, :MOBA_TOPK, :N_HEADS].transpose(0, 2, 1)
            o8 = moba_decode_attn(sel, page_table, rel_bias, q_s, k_s, v_s, cache_k, cache_v, n_seq=DB)
            o_s = jnp.pad(o8[:, 0, :].reshape(DB, D), ((0, DEC_ROWS - DB), (0, 0))).astype(BF16)
            xp, xs = mm(o, o_s, [(w_o_b, j, 0)], D, residual=(xp, xs), name="oproj")
        ids = page_ids[l * ids_per_layer:(l + 1) * ids_per_layer] if l < n_a else None
        up = mm(xp, xs, [(w_gate_b, l, 0), (w_up_b, l, 0)], F, gain=norm_ffn_g[l], epilogue="swiglu",
                outs=((BF16, None),), side_dtypes=(BF16,),
                page_stream=None if ids is None else (cache_k, ids), name="ffn_up")
        act, act_s = up[0], up[1]
        if ids is not None:
            page_sum_parts.append(up[2][:ids.shape[0]])
        xp, xs = mm(act, act_s, [(w_down_b, l, 0)], D, residual=(xp, xs), name="ffn_down")

    y_prompt = rmsnorm_rows(xp, final_norm_g, tm=ROW_TILE // 2).reshape(B, S, D)
    y_sample = rmsnorm_rows(xs, final_norm_g, tm=DEC_ROWS)[:DB].reshape(DB, 1, D)
    return (y_prompt, y_sample, jnp.stack(conv_p), jnp.stack(conv_s),
            k_heads.reshape(B, S, N_HEADS, hd), v_heads.reshape(B, S, N_HEADS, hd),
            k_s[:DB].reshape(DB, 1, N_HEADS, hd), v_s[:DB].reshape(DB, 1, N_HEADS, hd))
```

```python
import functools
import math

import numpy as np
import jax
import jax.numpy as jnp
from jax import lax
from jax.experimental import pallas as pl
from jax.experimental.pallas import tpu as pltpu

F32 = jnp.float32
BF16 = jnp.bfloat16

N_HEADS = 16
CONV_WIDTH = 31
MOBA_BLOCK = 256
MOBA_TOPK = 3
NUM_BUCKETS = 32
MAX_DISTANCE = 128
PAGE_SIZE = 128
EPS = 1e-6
NEG = -1e30
LOG2E = math.log2(math.e)

V7X_VMEM_BYTES = 64 * 1024 * 1024
V7X_LANES = 128
V7X_SUBLANES = 8
VMEM_CAP = V7X_VMEM_BYTES - 4 * 1024 * 1024


def _vmem_limit(est_bytes):
    return int(min(max(est_bytes + (8 << 20), 16 << 20), VMEM_CAP))


def _bucket_thresholds():
    max_exact = NUM_BUCKETS // 2
    n = np.arange(0, 4 * MAX_DISTANCE, dtype=np.int64)
    nf = np.maximum(n, max_exact).astype(np.float32)
    large = max_exact + (np.log(nf / np.float32(max_exact)) / np.float32(math.log(MAX_DISTANCE / max_exact))
                         * np.float32(NUM_BUCKETS - max_exact)).astype(np.int32)
    large = np.minimum(large, NUM_BUCKETS - 1)
    bucket = np.where(n < max_exact, n, large)
    assert np.all(np.diff(bucket) >= 0) and bucket[-1] == NUM_BUCKETS - 1
    return [int(np.argmax(bucket >= k)) for k in range(1, NUM_BUCKETS)]


_BUCKET_START = _bucket_thresholds()


def _bias_from_distance(dist, table_at):
    bias = jnp.full(dist.shape, table_at(0), F32)
    for k in range(1, NUM_BUCKETS):
        bias = jnp.where(dist >= _BUCKET_START[k - 1], table_at(k), bias)
    return bias


def _mm_body(*refs, has_norm, n_w, has_bias, has_res, has_side, epilogue, out_scales, n_outs, n_pages):
    it = iter(refs)
    if n_pages:
        next(it)
    x_refs = [next(it) for _ in range(1 + has_side)]
    g_ref = next(it) if has_norm else None
    w_refs = [next(it) for _ in range(n_w)]
    b_refs = [next(it) for _ in range(n_w)] if has_bias else []
    res_refs = [next(it) for _ in range(1 + has_side)] if has_res else [None, None]
    page_refs = [next(it) for _ in range(n_pages)]
    out_refs = [[next(it) for _ in range(n)] for n in n_outs]
    page_out = next(it) if n_pages else None
    if has_norm:
        xn_refs = [next(it) for _ in x_refs]

        @pl.when(pl.program_id(1) == 0)
        def _():
            for x_ref, xn_ref in zip(x_refs, xn_refs):
                x = x_ref[...]
                ms = jnp.mean(x * x, axis=-1, keepdims=True)
                xn_ref[...] = (x * lax.rsqrt(ms + EPS) * g_ref[...]).astype(xn_ref.dtype)

        x_refs = xn_refs

    def stream(x_ref, res_ref, outs, out_scale):
        xb = x_ref[...]
        accs = [jnp.dot(xb, w[...], preferred_element_type=F32) for w in w_refs]
        if has_bias:
            accs = [a + b[...] for a, b in zip(accs, b_refs)]
        if epilogue == "glu":
            y = accs[0] * jax.nn.sigmoid(accs[1])
        elif epilogue == "swiglu":
            y = jax.nn.silu(accs[0]) * accs[1]
        else:
            y = accs[0]
        if out_scale is not None:
            y = y * out_scale
        if has_res:
            y = res_ref[...] + y
        for o in outs:
            if len(o.shape) == 3:
                hd = o.shape[-1]
                for h in range(o.shape[1]):
                    o[:, h, :] = y[:, h * hd:(h + 1) * hd].astype(o.dtype)
            else:
                o[...] = y.astype(o.dtype)

    stream(x_refs[0], res_refs[0], out_refs[0], out_scales[0])
    if has_side:
        @pl.when(pl.program_id(0) == 0)
        def _():
            stream(x_refs[1], res_refs[1], out_refs[1], out_scales[1])

    for k, page_ref in enumerate(page_refs):
        page_out[k] = jnp.sum(page_ref[...], axis=0)


MATMUL_VMEM_BUDGET = 48 * 1024 * 1024


def fused_matmul(x, ws, n_cols, *, gain=None, biases=None, residual=None, epilogue="none",
                 outs=((F32, None),), out_scale=None, side=None, page_stream=None, tm, name):
    M, K = x.shape
    has_norm = gain is not None
    has_side = side is not None
    out_bytes = sum(jnp.dtype(dt).itemsize for dt, _ in outs) + (4 if residual is not None else 0)

    def pages_per_step(tn):
        return -(-page_stream[1].shape[0] // ((M // tm) * (n_cols // tn))) if page_stream is not None else 0

    def estimate(tn):
        page_bytes = 2 * pages_per_step(tn) * math.prod(page_stream[0].shape[1:]) * 4 if page_stream else 0
        return (2 * tm * K * x.dtype.itemsize + (tm * K * 2 if has_norm else 0) + 2 * len(ws) * K * tn * 2
                + 2 * tm * tn * out_bytes + (len(ws) + 1) * tm * tn * 4 + page_bytes)

    def legal(tn):
        heads_ok = all(hd is None or (tn // hd) % V7X_SUBLANES == 0 or tn == n_cols for _, hd in outs)
        offs_ok = all(off % tn == 0 for _, _, off in ws)
        return n_cols % tn == 0 and heads_ok and offs_ok

    tiles = [t for t in (2048, 1024, 512, 256, 128) if legal(t)]
    fitting = [t for t in tiles if estimate(t) <= MATMUL_VMEM_BUDGET]
    tn = fitting[0] if fitting else tiles[-1]
    assert M % tm == 0
    nj = n_cols // tn
    grid = (M // tm, nj)
    in_specs = [pl.BlockSpec((tm, K), lambda i, j, *_: (i, 0))]
    args = [x]
    if has_side:
        x_side, res_side, side_dtypes, side_scale = side
        R = x_side.shape[0]
        side_block = lambda i, j, *_: (0, jnp.where(i == 0, j, nj - 1))
        in_specs.append(pl.BlockSpec((R, K), lambda i, j, *_: (0, 0)))
        args.append(x_side)
    if has_norm:
        in_specs.append(pl.BlockSpec((1, K), lambda i, j, *_: (0, 0)))
        args.append(gain.reshape(1, K))
    for w, layer, off in ws:
        in_specs.append(pl.BlockSpec((None, K, tn), lambda i, j, *_, layer=layer, ob=off // tn: (layer, 0, j + ob)))
        args.append(w)
    if biases is not None:
        for b, off in biases:
            in_specs.append(pl.BlockSpec((1, tn), lambda i, j, *_, ob=off // tn: (0, j + ob)))
            args.append(b)
    if residual is not None:
        in_specs.append(pl.BlockSpec((tm, tn), lambda i, j, *_: (i, j)))
        args.append(residual)
        if has_side:
            in_specs.append(pl.BlockSpec((R, tn), side_block))
            args.append(res_side)
    out_specs, out_shape = [], []
    for dt, hd in outs:
        if hd is None:
            out_specs.append(pl.BlockSpec((tm, tn), lambda i, j, *_: (i, j)))
            out_shape.append(jax.ShapeDtypeStruct((M, n_cols), dt))
        else:
            out_specs.append(pl.BlockSpec((tm, tn // hd, hd), lambda i, j, *_: (i, j, 0)))
            out_shape.append(jax.ShapeDtypeStruct((M, n_cols // hd, hd), dt))
    n_outs, out_scales = [len(outs)], [out_scale]
    if has_side:
        for dt in side_dtypes:
            out_specs.append(pl.BlockSpec((R, tn), side_block))
            out_shape.append(jax.ShapeDtypeStruct((R, n_cols), dt))
        n_outs.append(len(side_dtypes))
        out_scales.append(side_scale)
    prefetch = []
    per_step = pages_per_step(tn)
    if page_stream is not None:
        cache, ids = page_stream
        n_slots = grid[0] * nj * per_step
        assert ids.shape[0] <= n_slots
        prefetch = [jnp.concatenate([ids, jnp.broadcast_to(ids[-1:], (n_slots - ids.shape[0],))])]
        for k in range(per_step):
            in_specs.append(pl.BlockSpec(
                (None,) + cache.shape[1:],
                lambda i, j, ids_ref, k=k: (ids_ref[(i * nj + j) * per_step + k], 0, 0, 0)))
            args.append(cache)
        out_specs.append(pl.BlockSpec((per_step,) + cache.shape[2:], lambda i, j, *_: (i * nj + j, 0, 0)))
        out_shape.append(jax.ShapeDtypeStruct((n_slots,) + cache.shape[2:], F32))
    scratch = []
    if has_norm:
        scratch = [pltpu.VMEM((tm, K), BF16)] + ([pltpu.VMEM((R, K), BF16)] if has_side else [])
    body = functools.partial(_mm_body, has_norm=has_norm, n_w=len(ws), has_bias=biases is not None,
                             has_res=residual is not None, has_side=has_side, epilogue=epilogue,
                             out_scales=out_scales, n_outs=n_outs, n_pages=per_step)
    res = pl.pallas_call(
        body,
        grid_spec=pltpu.PrefetchScalarGridSpec(num_scalar_prefetch=len(prefetch), grid=grid, in_specs=in_specs,
                                               out_specs=out_specs, scratch_shapes=scratch),
        out_shape=out_shape, name=name,
        compiler_params=pltpu.CompilerParams(dimension_semantics=("arbitrary", "arbitrary"),
                                             vmem_limit_bytes=_vmem_limit(estimate(tn))),
    )(*prefetch, *args)
    return res if len(res) > 1 else res[0]


def _rmsnorm_body(x_ref, g_ref, o_ref):
    x = x_ref[...]
    ms = jnp.mean(x * x, axis=-1, keepdims=True)
    o_ref[...] = x * lax.rsqrt(ms + EPS) * g_ref[...]


def rmsnorm_rows(x, gain, *, tm):
    M, D = x.shape
    return pl.pallas_call(
        _rmsnorm_body, grid=(M // tm,),
        in_specs=[pl.BlockSpec((tm, D), lambda i: (i, 0)), pl.BlockSpec((1, D), lambda i: (0, 0))],
        out_specs=pl.BlockSpec((tm, D), lambda i: (i, 0)),
        out_shape=jax.ShapeDtypeStruct((M, D), F32), name="rmsnorm_rows",
        compiler_params=pltpu.CompilerParams(dimension_semantics=("arbitrary",),
                                             vmem_limit_bytes=_vmem_limit(4 * tm * D * 4)),
    )(x, gain.reshape(1, D))


CONV_HALO = 32
CONV_ROWS = 64


def _ln_silu(c, g, b):
    mu = jnp.mean(c, axis=-1, keepdims=True)
    xc = c - mu
    var = jnp.mean(xc * xc, axis=-1, keepdims=True)
    y = xc * lax.rsqrt(var + EPS) * g + b
    return y * jax.nn.sigmoid(y)


def _conv_body(u_ref, halo_ref, left_ref, w_ref, bdw_ref, g_ref, b_ref, o_ref, ext_ref, y_ref, *, tt):
    first = pl.program_id(1) == 0
    ext_ref[0:CONV_HALO, :] = jnp.where(first, left_ref[...], halo_ref[...])
    ext_ref[CONV_HALO:, :] = u_ref[...]
    d_model = u_ref.shape[-1]
    lead = CONV_HALO - (CONV_WIDTH - 1)

    def chunk(c, carry):
        lanes = pl.ds(pl.multiple_of(c * V7X_LANES, V7X_LANES), V7X_LANES)
        w = w_ref[:, lanes]
        for r0 in range(0, tt, CONV_ROWS):
            n_win = CONV_ROWS + CONV_HALO
            win = ext_ref[r0:r0 + n_win, lanes]
            acc = jnp.zeros((CONV_ROWS, V7X_LANES), F32)
            for res in range(V7X_SUBLANES):
                shifted = win if res == 0 else pltpu.roll(win, n_win - res, axis=0)
                for off in range(res, lead + CONV_WIDTH, V7X_SUBLANES):
                    k = off - lead
                    if k >= 0:
                        acc = acc + shifted[off - res:off - res + CONV_ROWS, :] * w[k:k + 1, :]
            y_ref[r0:r0 + CONV_ROWS, lanes] = acc
        return carry

    lax.fori_loop(0, d_model // V7X_LANES, chunk, 0)
    c = y_ref[...] + bdw_ref[...]
    o_ref[...] = _ln_silu(c, g_ref[...], b_ref[...]).astype(o_ref.dtype)


def conv_ln_silu(u, left, w_dw, b_dw, ln_g, ln_b, *, tt):
    B, T, D = u.shape
    left_pad = jnp.pad(left, ((0, 0), (CONV_HALO - (CONV_WIDTH - 1), 0), (0, 0)))
    hb = tt // CONV_HALO
    row = lambda a: a.reshape(1, D)
    est = 2 * tt * D * 4 + 4 * CONV_HALO * D * 4 + (2 * tt + CONV_HALO) * D * 4 + 2 * tt * D * 2 + 6 * tt * D * 4
    return pl.pallas_call(
        functools.partial(_conv_body, tt=tt), grid=(B, T // tt),
        in_specs=[
            pl.BlockSpec((None, tt, D), lambda b, i: (b, i, 0)),
            pl.BlockSpec((None, CONV_HALO, D), lambda b, i: (b, jnp.maximum(i * hb - 1, 0), 0)),
            pl.BlockSpec((None, CONV_HALO, D), lambda b, i: (b, 0, 0)),
            pl.BlockSpec((CONV_WIDTH, D), lambda b, i: (0, 0)),
            pl.BlockSpec((1, D), lambda b, i: (0, 0)),
            pl.BlockSpec((1, D), lambda b, i: (0, 0)),
            pl.BlockSpec((1, D), lambda b, i: (0, 0)),
        ],
        out_specs=pl.BlockSpec((None, tt, D), lambda b, i: (b, i, 0)),
        out_shape=jax.ShapeDtypeStruct((B, T, D), BF16),
        scratch_shapes=[pltpu.VMEM((tt + CONV_HALO, D), F32), pltpu.VMEM((tt, D), F32)],
        name="conv_ln_silu",
        compiler_params=pltpu.CompilerParams(dimension_semantics=("arbitrary", "arbitrary"),
                                             vmem_limit_bytes=_vmem_limit(est)),
    )(u, u, left_pad, w_dw, row(b_dw), row(ln_g), row(ln_b))


def _conv_step_body(state_ref, u_ref, w_ref, bdw_ref, g_ref, b_ref, o_ref):
    acc = u_ref[...] * w_ref[CONV_WIDTH - 1:CONV_WIDTH, :]
    for k in range(CONV_WIDTH - 1):
        acc = acc + state_ref[:, k, :] * w_ref[k:k + 1, :]
    c = acc + bdw_ref[...]
    o_ref[...] = _ln_silu(c, g_ref[...], b_ref[...]).astype(o_ref.dtype)


def conv_step(state, u_new, w_dw, b_dw, ln_g, ln_b):
    NB, _, D = state.shape
    row = lambda a: a.reshape(1, D)
    return pl.pallas_call(
        _conv_step_body,
        out_shape=jax.ShapeDtypeStruct((NB, D), BF16), name="conv_step",
    )(state, u_new, w_dw, row(b_dw), row(ln_g), row(ln_b))


def _relbias_body(table_ref, o_ref):
    h = pl.program_id(0)
    a = lax.broadcasted_iota(jnp.int32, (MOBA_BLOCK, MOBA_BLOCK), 0)
    b = lax.broadcasted_iota(jnp.int32, (MOBA_BLOCK, MOBA_BLOCK), 1)
    far = table_ref[NUM_BUCKETS - 1, h]
    table_at = lambda k: table_ref[k, h]
    d_own = a - b
    own = (_bias_from_distance(d_own, table_at) - far) * LOG2E
    o_ref[0] = jnp.where(d_own >= 0, own, NEG)
    o_ref[1] = (_bias_from_distance(d_own + MOBA_BLOCK, table_at) - far) * LOG2E


def relbias_tables(rel_bias):
    return pl.pallas_call(
        _relbias_body, grid=(N_HEADS,),
        in_specs=[pl.BlockSpec(memory_space=pltpu.SMEM)],
        out_specs=pl.BlockSpec((None, 2, MOBA_BLOCK, MOBA_BLOCK), lambda h: (h, 0, 0, 0)),
        out_shape=jax.ShapeDtypeStruct((N_HEADS, 2, MOBA_BLOCK, MOBA_BLOCK), F32), name="relbias_tables",
    )(rel_bias)


GATE_ROWS = 16


def _moba_prompt_body(q_ref, k_ref, v_ref, bias_ref, o_ref, kx_ref, vx_ref, s_ref, p_ref, acc_ref, *, nb):
    blk = MOBA_BLOCK
    hd = q_ref.shape[-1]
    seq = nb * blk
    nt = (((1,), (1,)), ((), ()))

    k = k_ref[...]
    kx_ref[:, 0:hd] = k
    kr = lax.broadcasted_iota(jnp.int32, (seq, V7X_LANES), 0)
    kc = lax.broadcasted_iota(jnp.int32, (seq, V7X_LANES), 1)
    kx_ref[:, hd:] = (kr // blk == kc).astype(BF16)
    vx_ref[:, 0:hd] = v_ref[...]
    vx_ref[:, hd:] = (kc == 0).astype(BF16)
    r = lax.broadcasted_iota(jnp.int32, (GATE_ROWS, seq), 0)
    c = lax.broadcasted_iota(jnp.int32, (GATE_ROWS, seq), 1)
    avg = jnp.where(c // blk == r, 1.0 / blk, 0.0).astype(BF16)
    km = jnp.dot(avg, k, preferred_element_type=F32)
    km_hi = km.astype(BF16)
    km_lo = (km - km_hi.astype(F32)).astype(BF16)

    q = q_ref[...]
    g2 = (lax.dot_general(km_hi, q, nt, preferred_element_type=F32)
          + lax.dot_general(km_lo, q, nt, preferred_element_type=F32))
    gate = g2[0:V7X_SUBLANES]
    row = lax.broadcasted_iota(jnp.int32, gate.shape, 0)
    own_blk = lax.broadcasted_iota(jnp.int32, gate.shape, 1) // blk
    gate = jnp.where(row < own_blk, gate, NEG)
    beaten = jnp.zeros(gate.shape, jnp.int32)
    for jp in range(nb):
        gj = gate[jp:jp + 1, :]
        beaten = beaten + ((gj > gate) | ((gj == gate) & (jp < row))).astype(jnp.int32)
    chosen = (beaten < MOBA_TOPK) & (row < own_blk)
    mask_t = jnp.where(chosen | (row == own_blk), 0.0, NEG)
    mask_t = jnp.concatenate([mask_t, jnp.zeros((V7X_LANES - V7X_SUBLANES, seq), F32)], axis=0)
    qx = jnp.concatenate([q, mask_t.T.astype(BF16)], axis=1)

    for j in range(nb):
        s_ref[j * blk:, j * blk:(j + 1) * blk] = lax.dot_general(
            qx[j * blk:, :], kx_ref[j * blk:(j + 1) * blk, :], nt, preferred_element_type=F32)
    for own in range(nb):
        n = (own + 1) * blk
        rows = slice(own * blk, n)
        parts = [s_ref[rows, n - blk:n] + bias_ref[0]]
        if own >= 1:
            parts.insert(0, s_ref[rows, n - 2 * blk:n - blk] + bias_ref[1])
        if own >= 2:
            parts.insert(0, s_ref[rows, 0:n - 2 * blk])
        s = jnp.concatenate(parts, axis=1) if len(parts) > 1 else parts[0]
        m = jnp.max(s, axis=-1, keepdims=True)
        p_ref[rows, 0:n] = jnp.exp2(s - m).astype(BF16)
    for j in range(nb):
        pv = jnp.dot(p_ref[j * blk:, j * blk:(j + 1) * blk], vx_ref[j * blk:(j + 1) * blk, :],
                     preferred_element_type=F32)
        if j == 0:
            acc_ref[...] = pv
        else:
            acc_ref[j * blk:, :] += pv
    acc = acc_ref[...]
    o_ref[...] = (acc[:, 0:hd] * (1.0 / acc[:, hd:hd + 1])).astype(o_ref.dtype)


def moba_prompt_attn(q, k, v, bias, *, n_seq, seq):
    M, HD = q.shape
    hd = HD // N_HEADS
    nb = seq // MOBA_BLOCK
    assert seq % MOBA_BLOCK == 0 and nb <= V7X_SUBLANES
    head_spec = pl.BlockSpec((seq, hd), lambda b, h: (b, h))
    return pl.pallas_call(
        functools.partial(_moba_prompt_body, nb=nb), grid=(n_seq, N_HEADS),
        in_specs=[head_spec, head_spec, head_spec,
                  pl.BlockSpec((None, 2, MOBA_BLOCK, MOBA_BLOCK), lambda b, h: (h, 0, 0, 0))],
        out_specs=head_spec,
        out_shape=jax.ShapeDtypeStruct((M, HD), BF16),
        scratch_shapes=[pltpu.VMEM((seq, hd + V7X_LANES), BF16), pltpu.VMEM((seq, hd + V7X_LANES), BF16),
                        pltpu.VMEM((seq, seq), F32), pltpu.VMEM((seq, seq), BF16),
                        pltpu.VMEM((seq, hd + V7X_LANES), F32)],
        name="moba_prompt_attn",
        compiler_params=pltpu.CompilerParams(dimension_semantics=("arbitrary", "arbitrary"),
                                             vmem_limit_bytes=_vmem_limit(seq * seq * 6 + 16 * seq * hd * 4)),
    )(q, k, v, bias)


def _gate_topk_body(q_ref, sums_ref, knew_ref, o_ref, *, nbp, ppb, own, inv_rows):
    b = pl.program_id(0)
    HD = q_ref.shape[-1]
    hd = HD // N_HEADS
    q = q_ref[pl.ds(b, 1), :]
    r = lax.broadcasted_iota(jnp.int32, (HD, V7X_LANES), 0)
    c = lax.broadcasted_iota(jnp.int32, (HD, V7X_LANES), 1)
    ind = (r // hd == c).astype(BF16)

    def head_sums(prod):
        hi = prod.astype(BF16)
        mid = (prod - hi.astype(F32)).astype(BF16)
        lo = (prod - hi.astype(F32) - mid.astype(F32)).astype(BF16)
        return (jnp.dot(hi, ind, preferred_element_type=F32) + jnp.dot(mid, ind, preferred_element_type=F32)
                + jnp.dot(lo, ind, preferred_element_type=F32))

    block_sums = sums_ref[:, 0:HD]
    for p in range(1, ppb):
        block_sums = block_sums + sums_ref[:, p * HD:(p + 1) * HD]
    means = block_sums * inv_rows
    gate_past = head_sums(means * q)
    k_new = jnp.broadcast_to(knew_ref[pl.ds(b, 1), :], (V7X_SUBLANES, HD))
    gate_rec = head_sums((k_new * inv_rows) * q)
    gate = jnp.concatenate([gate_past, gate_rec], axis=0)
    idx = lax.broadcasted_iota(jnp.int32, gate.shape, 0)
    gate = jnp.where(idx < own, gate, NEG)
    gate = jnp.where(idx <= nbp, gate, -jnp.inf)
    picks = []
    for _ in range(MOBA_TOPK):
        mx = jnp.max(gate, axis=0, keepdims=True)
        pick = jnp.min(jnp.where(gate == mx, idx, nbp + V7X_SUBLANES), axis=0, keepdims=True)
        picks.append(pick)
        gate = jnp.where(idx == pick, -jnp.inf, gate)
    picks.append(jnp.zeros((V7X_SUBLANES - MOBA_TOPK, V7X_LANES), jnp.int32))
    o_ref[...] = jnp.concatenate(picks, axis=0)


def moba_gate_topk(q_rows, page_sums, k_new_rows, *, n_seq, ppb):
    R, HD = q_rows.shape
    _, nbp, _ = page_sums.shape
    return pl.pallas_call(
        functools.partial(_gate_topk_body, nbp=nbp, ppb=ppb, own=nbp, inv_rows=1.0 / MOBA_BLOCK), grid=(n_seq,),
        in_specs=[
            pl.BlockSpec((R, HD), lambda b: (0, 0)),
            pl.BlockSpec((None, nbp, ppb * HD), lambda b: (b, 0, 0)),
            pl.BlockSpec((R, HD), lambda b: (0, 0)),
        ],
        out_specs=pl.BlockSpec((None, V7X_SUBLANES, V7X_LANES), lambda b: (b, 0, 0)),
        out_shape=jax.ShapeDtypeStruct((n_seq, V7X_SUBLANES, V7X_LANES), jnp.int32), name="moba_gate_topk",
        compiler_params=pltpu.CompilerParams(dimension_semantics=("arbitrary",)),
    )(q_rows, page_sums, k_new_rows)


DEC_HEADS_PER_STEP = 4


def _decode_attn_body(sel_ref, pt_ref, table_ref, q_ref, kn_ref, vn_ref, ck_ref, cv_ref, o_ref,
                      kbuf, vbuf, sem, *, nbp, ppb, t_new, scale):
    step = pl.program_id(0)
    hd = kbuf.shape[-1]
    hg = DEC_HEADS_PER_STEP
    groups = N_HEADS // hg
    n_buf = MOBA_TOPK * ppb

    def page_copies(st, slot):
        b = st // groups
        out = []
        for hh in range(hg):
            h = (st % groups) * hg + hh
            for kk in range(MOBA_TOPK):
                blk = jnp.minimum(sel_ref[b, h, kk], nbp - 1)
                for p in range(ppb):
                    page = pt_ref[b, blk * ppb + p]
                    i = hh * n_buf + kk * ppb + p
                    out.append(pltpu.make_async_copy(ck_ref.at[page, :, h, :], kbuf.at[slot, i], sem.at[0, slot, i]))
                    out.append(pltpu.make_async_copy(cv_ref.at[page, :, h, :], vbuf.at[slot, i], sem.at[1, slot, i]))
        return out

    slot = step % 2

    @pl.when(step == 0)
    def _():
        for c in page_copies(step, slot):
            c.start()

    @pl.when(step + 1 < pl.num_programs(0))
    def _():
        for c in page_copies(step + 1, 1 - slot):
            c.start()

    for c in page_copies(step, slot):
        c.wait()

    b = step // groups
    nt = (((1,), (1,)), ((), ()))
    for hh in range(hg):
        h = (step % groups) * hg + hh
        lanes = pl.ds(pl.multiple_of(h * hd, hd), hd)

        def rows8(ref):
            x = ref[:, lanes]
            r = lax.broadcasted_iota(jnp.int32, x.shape, 0)
            row = jnp.sum(jnp.where(r == b, x, 0.0), axis=0, keepdims=True)
            return jnp.broadcast_to(row, (V7X_SUBLANES, hd))

        q = rows8(q_ref)
        qb = q.astype(BF16)
        table_at = lambda k, h=h: table_ref[k, h]
        kn, vn = rows8(kn_ref), rows8(vn_ref)
        l_new = jnp.sum(q * kn, axis=-1, keepdims=True) * scale + table_at(0)
        logits, values = [], []
        for kk in range(MOBA_TOPK):
            sel = sel_ref[b, h, kk]
            valid = sel < nbp
            selp = jnp.minimum(sel, nbp - 1)
            for p in range(ppb):
                i = hh * n_buf + kk * ppb + p
                kp = kbuf[slot, i].astype(BF16)
                s = lax.dot_general(qb, kp, nt, preferred_element_type=F32) * scale
                pos = selp * MOBA_BLOCK + p * PAGE_SIZE + lax.broadcasted_iota(jnp.int32, s.shape, 1)
                s = s + _bias_from_distance(t_new - pos, table_at)
                logits.append(jnp.where(valid, s, NEG))
                values.append(vbuf[slot, i].astype(BF16))
        m = l_new
        for s in logits:
            m = jnp.maximum(m, jnp.max(s, axis=-1, keepdims=True))
        p_new = jnp.exp(l_new - m)
        den = p_new
        acc = p_new * vn
        for s, v in zip(logits, values):
            p = jnp.exp(s - m)
            den = den + jnp.sum(p, axis=-1, keepdims=True)
            acc = acc + jnp.dot(p.astype(BF16), v, preferred_element_type=F32)
        o_ref[hh] = acc / den


def moba_decode_attn(sel, page_table, rel_bias, q_rows, k_new_rows, v_new_rows, cache_k, cache_v, *, n_seq):
    R, HD = q_rows.shape
    hd = HD // N_HEADS
    n_pages = page_table.shape[1]
    ppb = MOBA_BLOCK // PAGE_SIZE
    nbp = n_pages // ppb
    hg = DEC_HEADS_PER_STEP
    assert n_pages % ppb == 0, "past length must be a whole number of MoBA blocks"
    assert N_HEADS % hg == 0
    row_spec = pl.BlockSpec((R, HD), lambda s, sel, pt: (0, 0))
    n_buf = hg * MOBA_TOPK * ppb
    body = functools.partial(_decode_attn_body, nbp=nbp, ppb=ppb, t_new=n_pages * PAGE_SIZE, scale=hd ** -0.5)
    return pl.pallas_call(
        body,
        grid_spec=pltpu.PrefetchScalarGridSpec(
            num_scalar_prefetch=2, grid=(n_seq * N_HEADS // hg,),
            in_specs=[pl.BlockSpec(memory_space=pltpu.SMEM), row_spec, row_spec, row_spec,
                      pl.BlockSpec(memory_space=pl.ANY), pl.BlockSpec(memory_space=pl.ANY)],
            out_specs=pl.BlockSpec((hg, V7X_SUBLANES, hd), lambda s, sel, pt: (s, 0, 0)),
            scratch_shapes=[pltpu.VMEM((2, n_buf, PAGE_SIZE, hd), F32), pltpu.VMEM((2, n_buf, PAGE_SIZE, hd), F32),
                            pltpu.SemaphoreType.DMA((2, 2, n_buf))],
        ),
        out_shape=jax.ShapeDtypeStruct((n_seq * N_HEADS, V7X_SUBLANES, hd), F32), name="moba_decode_attn",
        compiler_params=pltpu.CompilerParams(dimension_semantics=("arbitrary",)),
    )(sel, page_table, rel_bias, q_rows, k_new_rows, v_new_rows, cache_k, cache_v)


ROW_TILE = 1024
DEC_ROWS = 16
CONV_TIME_TILE = 256


def kernel(x_prompt, x_sample, state_conv, cache_k, cache_v, page_table, norm_mix_g, norm_ffn_g, a_w_in, a_b_in, a_w_dw, a_b_dw, a_ln_g, a_ln_b, a_w_out, a_b_out, kv_norm_g, w_k, w_v, b_w_q, b_w_o, rel_bias, f_w_gate, f_w_up, f_w_down, final_norm_g):
    B, S, D = x_prompt.shape
    DB, dec_seq, _ = x_sample.shape
    assert dec_seq == 1, "decode path handles one new token per sequence"
    n_a = a_w_in.shape[0]
    depth = norm_mix_g.shape[0]
    F = f_w_gate.shape[-1]
    hd = D // N_HEADS
    ppb = MOBA_BLOCK // PAGE_SIZE
    n_pages = page_table.shape[1]

    bf = lambda w: w.astype(BF16)
    a_w_in_b, a_w_out_b = bf(a_w_in), bf(a_w_out)
    w_k_b, w_v_b, w_q_b, w_o_b = bf(w_k)[None], bf(w_v)[None], bf(b_w_q), bf(b_w_o)
    w_gate_b, w_up_b, w_down_b = bf(f_w_gate), bf(f_w_up), bf(f_w_down)

    xp = x_prompt.reshape(B * S, D)
    xs = jnp.pad(x_sample.reshape(DB, D), ((0, DEC_ROWS - DB), (0, 0)))

    def mm(x, x_side, ws, n_cols, *, residual=None, outs=((F32, None),), side_dtypes=(F32,), out_scale=None,
           side_scale=None, **kw):
        res_main, res_side = residual if residual is not None else (None, None)
        return fused_matmul(x, ws, n_cols, residual=res_main, outs=outs, out_scale=out_scale,
                            side=(x_side, res_side, side_dtypes, side_scale), tm=ROW_TILE, **kw)

    conv_p, conv_s = [], []
    bias_tables = relbias_tables(rel_bias)
    page_ids = page_table.reshape(-1)
    ids_per_layer = -(-page_ids.shape[0] // n_a)
    page_sum_parts = []
    for l in range(depth):
        if l < n_a:
            b_in = a_b_in[l].reshape(1, 2 * D)
            u, u_s = mm(xp, xs, [(a_w_in_b, l, 0), (a_w_in_b, l, D)], D, gain=norm_mix_g[l],
                        biases=[(b_in, 0), (b_in, D)], epilogue="glu", name="glu")
            u3 = u.reshape(B, S, D)
            left = jnp.zeros((B, CONV_WIDTH - 1, D), F32)
            c = conv_ln_silu(u3, left, a_w_dw[l], a_b_dw[l], a_ln_g[l], a_ln_b[l],
                             tt=CONV_TIME_TILE).reshape(B * S, D)
            conv_p.append(u3[:, S - (CONV_WIDTH - 1):, :])
            c_s = conv_step(state_conv[l], u_s[:DB], a_w_dw[l], a_b_dw[l], a_ln_g[l], a_ln_b[l])
            c_s = jnp.pad(c_s, ((0, DEC_ROWS - DB), (0, 0)))
            conv_s.append(jnp.concatenate([state_conv[l][:, 1:, :], u_s[:DB, None, :]], axis=1))
            xp, xs = mm(c, c_s, [(a_w_out_b, l, 0)], D, biases=[(a_b_out[l].reshape(1, D), 0)],
                        residual=(xp, xs), name="convout")
        else:
            j = l - n_a
            if l == n_a:
                kv_outs = ((F32, hd), (BF16, None))
                k_heads, k_b, k_s = mm(xp, xs, [(w_k_b, 0, 0)], D, gain=kv_norm_g, outs=kv_outs, name="kproj")
                v_heads, v_b, v_s = mm(xp, xs, [(w_v_b, 0, 0)], D, gain=kv_norm_g, outs=kv_outs, name="vproj")
                page_sums = jnp.concatenate(page_sum_parts).reshape(DB, n_pages // ppb, ppb * D)
            q, q_s = mm(xp, xs, [(w_q_b, j, 0)], D, gain=norm_mix_g[l], outs=((BF16, None),),
                        out_scale=hd ** -0.5 * LOG2E, name="qproj")
            o = moba_prompt_attn(q, k_b, v_b, bias_tables, n_seq=B, seq=S)
            picks = moba_gate_topk(q_s, page_sums, k_s, n_seq=DB, ppb=ppb)
            sel = picks[:, :MOBA_TOPK, :N_HEADS].transpose(0, 2, 1)
            o8 = moba_decode_attn(sel, page_table, rel_bias, q_s, k_s, v_s, cache_k, cache_v, n_seq=DB)
            o_s = jnp.pad(o8[:, 0, :].reshape(DB, D), ((0, DEC_ROWS - DB), (0, 0))).astype(BF16)
            xp, xs = mm(o, o_s, [(w_o_b, j, 0)], D, residual=(xp, xs), name="oproj")
        ids = page_ids[l * ids_per_layer:(l + 1) * ids_per_layer] if l < n_a else None
        up = mm(xp, xs, [(w_gate_b, l, 0), (w_up_b, l, 0)], F, gain=norm_ffn_g[l], epilogue="swiglu",
                outs=((BF16, None),), side_dtypes=(BF16,),
                page_stream=None if ids is None else (cache_k, ids), name="ffn_up")
        act, act_s = up[0], up[1]
        if ids is not None:
            page_sum_parts.append(up[2][:ids.shape[0]])
        xp, xs = mm(act, act_s, [(w_down_b, l, 0)], D, residual=(xp, xs), name="ffn_down")

    y_prompt = rmsnorm_rows(xp, final_norm_g, tm=ROW_TILE // 2).reshape(B, S, D)
    y_sample = rmsnorm_rows(xs, final_norm_g, tm=DEC_ROWS)[:DB].reshape(DB, 1, D)
    return (y_prompt, y_sample, jnp.stack(conv_p), jnp.stack(conv_s),
            k_heads.reshape(B, S, N_HEADS, hd), v_heads.reshape(B, S, N_HEADS, hd),
            k_s[:DB].reshape(DB, 1, N_HEADS, hd), v_s[:DB].reshape(DB, 1, N_HEADS, hd))
```

```python
import functools
import math

import numpy as np
import jax
import jax.numpy as jnp
from jax import lax
from jax.experimental import pallas as pl
from jax.experimental.pallas import tpu as pltpu

F32 = jnp.float32
BF16 = jnp.bfloat16

N_HEADS = 16
CONV_WIDTH = 31
MOBA_BLOCK = 256
MOBA_TOPK = 3
NUM_BUCKETS = 32
MAX_DISTANCE = 128
PAGE_SIZE = 128
EPS = 1e-6
NEG = -1e30
LOG2E = math.log2(math.e)

V7X_VMEM_BYTES = 64 * 1024 * 1024
V7X_LANES = 128
V7X_SUBLANES = 8
VMEM_CAP = V7X_VMEM_BYTES - 4 * 1024 * 1024


def _vmem_limit(est_bytes):
    return int(min(max(est_bytes + (8 << 20), 16 << 20), VMEM_CAP))


def _bucket_thresholds():
    max_exact = NUM_BUCKETS // 2
    n = np.arange(0, 4 * MAX_DISTANCE, dtype=np.int64)
    nf = np.maximum(n, max_exact).astype(np.float32)
    large = max_exact + (np.log(nf / np.float32(max_exact)) / np.float32(math.log(MAX_DISTANCE / max_exact))
                         * np.float32(NUM_BUCKETS - max_exact)).astype(np.int32)
    large = np.minimum(large, NUM_BUCKETS - 1)
    bucket = np.where(n < max_exact, n, large)
    assert np.all(np.diff(bucket) >= 0) and bucket[-1] == NUM_BUCKETS - 1
    return [int(np.argmax(bucket >= k)) for k in range(1, NUM_BUCKETS)]


_BUCKET_START = _bucket_thresholds()


def _bias_from_distance(dist, table_at):
    bias = jnp.full(dist.shape, table_at(0), F32)
    for k in range(1, NUM_BUCKETS):
        bias = jnp.where(dist >= _BUCKET_START[k - 1], table_at(k), bias)
    return bias


def _mm_body(*refs, has_norm, n_w, has_bias, has_res, has_side, epilogue, out_scales, n_outs, n_pages):
    it = iter(refs)
    if n_pages:
        next(it)
    x_refs = [next(it) for _ in range(1 + has_side)]
    g_ref = next(it) if has_norm else None
    w_refs = [next(it) for _ in range(n_w)]
    b_refs = [next(it) for _ in range(n_w)] if has_bias else []
    res_refs = [next(it) for _ in range(1 + has_side)] if has_res else [None, None]
    page_refs = [next(it) for _ in range(n_pages)]
    out_refs = [[next(it) for _ in range(n)] for n in n_outs]
    page_out = next(it) if n_pages else None
    if has_norm:
        xn_refs = [next(it) for _ in x_refs]

        @pl.when(pl.program_id(1) == 0)
        def _():
            for x_ref, xn_ref in zip(x_refs, xn_refs):
                x = x_ref[...]
                ms = jnp.mean(x * x, axis=-1, keepdims=True)
                xn_ref[...] = (x * lax.rsqrt(ms + EPS) * g_ref[...]).astype(xn_ref.dtype)

        x_refs = xn_refs

    def stream(x_ref, res_ref, outs, out_scale):
        xb = x_ref[...]
        accs = [jnp.dot(xb, w[...], preferred_element_type=F32) for w in w_refs]
        if has_bias:
            accs = [a + b[...] for a, b in zip(accs, b_refs)]
        if epilogue == "glu":
            y = accs[0] * jax.nn.sigmoid(accs[1])
        elif epilogue == "swiglu":
            y = jax.nn.silu(accs[0]) * accs[1]
        else:
            y = accs[0]
        if out_scale is not None:
            y = y * out_scale
        if has_res:
            y = res_ref[...] + y
        for o in outs:
            if len(o.shape) == 3:
                hd = o.shape[-1]
                for h in range(o.shape[1]):
                    o[:, h, :] = y[:, h * hd:(h + 1) * hd].astype(o.dtype)
            else:
                o[...] = y.astype(o.dtype)

    stream(x_refs[0], res_refs[0], out_refs[0], out_scales[0])
    if has_side:
        @pl.when(pl.program_id(0) == 0)
        def _():
            stream(x_refs[1], res_refs[1], out_refs[1], out_scales[1])

    for k, page_ref in enumerate(page_refs):
        page_out[k] = jnp.sum(page_ref[...], axis=0)


MATMUL_VMEM_BUDGET = 48 * 1024 * 1024


def fused_matmul(x, ws, n_cols, *, gain=None, biases=None, residual=None, epilogue="none",
                 outs=((F32, None),), out_scale=None, side=None, page_stream=None, tm, name):
    M, K = x.shape
    has_norm = gain is not None
    has_side = side is not None
    out_bytes = sum(jnp.dtype(dt).itemsize for dt, _ in outs) + (4 if residual is not None else 0)

    def pages_per_step(tn):
        return -(-page_stream[1].shape[0] // ((M // tm) * (n_cols // tn))) if page_stream is not None else 0

    def estimate(tn):
        page_bytes = 2 * pages_per_step(tn) * math.prod(page_stream[0].shape[1:]) * 4 if page_stream else 0
        return (2 * tm * K * x.dtype.itemsize + (tm * K * 2 if has_norm else 0) + 2 * len(ws) * K * tn * 2
                + 2 * tm * tn * out_bytes + (len(ws) + 1) * tm * tn * 4 + page_bytes)

    def legal(tn):
        heads_ok = all(hd is None or (tn // hd) % V7X_SUBLANES == 0 or tn == n_cols for _, hd in outs)
        offs_ok = all(off % tn == 0 for _, _, off in ws)
        return n_cols % tn == 0 and heads_ok and offs_ok

    tiles = [t for t in (2048, 1024, 512, 256, 128) if legal(t)]
    fitting = [t for t in tiles if estimate(t) <= MATMUL_VMEM_BUDGET]
    tn = fitting[0] if fitting else tiles[-1]
    assert M % tm == 0
    nj = n_cols // tn
    grid = (M // tm, nj)
    in_specs = [pl.BlockSpec((tm, K), lambda i, j, *_: (i, 0))]
    args = [x]
    if has_side:
        x_side, res_side, side_dtypes, side_scale = side
        R = x_side.shape[0]
        side_block = lambda i, j, *_: (0, jnp.where(i == 0, j, nj - 1))
        in_specs.append(pl.BlockSpec((R, K), lambda i, j, *_: (0, 0)))
        args.append(x_side)
    if has_norm:
        in_specs.append(pl.BlockSpec((1, K), lambda i, j, *_: (0, 0)))
        args.append(gain.reshape(1, K))
    for w, layer, off in ws:
        in_specs.append(pl.BlockSpec((None, K, tn), lambda i, j, *_, layer=layer, ob=off // tn: (layer, 0, j + ob)))
        args.append(w)
    if biases is not None:
        for b, off in biases:
            in_specs.append(pl.BlockSpec((1, tn), lambda i, j, *_, ob=off // tn: (0, j + ob)))
            args.append(b)
    if residual is not None:
        in_specs.append(pl.BlockSpec((tm, tn), lambda i, j, *_: (i, j)))
        args.append(residual)
        if has_side:
            in_specs.append(pl.BlockSpec((R, tn), side_block))
            args.append(res_side)
    out_specs, out_shape = [], []
    for dt, hd in outs:
        if hd is None:
            out_specs.append(pl.BlockSpec((tm, tn), lambda i, j, *_: (i, j)))
            out_shape.append(jax.ShapeDtypeStruct((M, n_cols), dt))
        else:
            out_specs.append(pl.BlockSpec((tm, tn // hd, hd), lambda i, j, *_: (i, j, 0)))
            out_shape.append(jax.ShapeDtypeStruct((M, n_cols // hd, hd), dt))
    n_outs, out_scales = [len(outs)], [out_scale]
    if has_side:
        for dt in side_dtypes:
            out_specs.append(pl.BlockSpec((R, tn), side_block))
            out_shape.append(jax.ShapeDtypeStruct((R, n_cols), dt))
        n_outs.append(len(side_dtypes))
        out_scales.append(side_scale)
    prefetch = []
    per_step = pages_per_step(tn)
    if page_stream is not None:
        cache, ids = page_stream
        n_slots = grid[0] * nj * per_step
        assert ids.shape[0] <= n_slots
        prefetch = [jnp.concatenate([ids, jnp.broadcast_to(ids[-1:], (n_slots - ids.shape[0],))])]
        for k in range(per_step):
            in_specs.append(pl.BlockSpec(
                (None,) + cache.shape[1:],
                lambda i, j, ids_ref, k=k: (ids_ref[(i * nj + j) * per_step + k], 0, 0, 0)))
            args.append(cache)
        out_specs.append(pl.BlockSpec((per_step,) + cache.shape[2:], lambda i, j, *_: (i * nj + j, 0, 0)))
        out_shape.append(jax.ShapeDtypeStruct((n_slots,) + cache.shape[2:], F32))
    scratch = []
    if has_norm:
        scratch = [pltpu.VMEM((tm, K), BF16)] + ([pltpu.VMEM((R, K), BF16)] if has_side else [])
    body = functools.partial(_mm_body, has_norm=has_norm, n_w=len(ws), has_bias=biases is not None,
                             has_res=residual is not None, has_side=has_side, epilogue=epilogue,
                             out_scales=out_scales, n_outs=n_outs, n_pages=per_step)
    res = pl.pallas_call(
        body,
        grid_spec=pltpu.PrefetchScalarGridSpec(num_scalar_prefetch=len(prefetch), grid=grid, in_specs=in_specs,
                                               out_specs=out_specs, scratch_shapes=scratch),
        out_shape=out_shape, name=name,
        compiler_params=pltpu.CompilerParams(dimension_semantics=("arbitrary", "arbitrary"),
                                             vmem_limit_bytes=_vmem_limit(estimate(tn))),
    )(*prefetch, *args)
    return res if len(res) > 1 else res[0]


def _rmsnorm_body(x_ref, g_ref, o_ref):
    x = x_ref[...]
    ms = jnp.mean(x * x, axis=-1, keepdims=True)
    o_ref[...] = x * lax.rsqrt(ms + EPS) * g_ref[...]


def rmsnorm_rows(x, gain, *, tm):
    M, D = x.shape
    return pl.pallas_call(
        _rmsnorm_body, grid=(M // tm,),
        in_specs=[pl.BlockSpec((tm, D), lambda i: (i, 0)), pl.BlockSpec((1, D), lambda i: (0, 0))],
        out_specs=pl.BlockSpec((tm, D), lambda i: (i, 0)),
        out_shape=jax.ShapeDtypeStruct((M, D), F32), name="rmsnorm_rows",
        compiler_params=pltpu.CompilerParams(dimension_semantics=("arbitrary",),
                                             vmem_limit_bytes=_vmem_limit(4 * tm * D * 4)),
    )(x, gain.reshape(1, D))


CONV_HALO = 32
CONV_ROWS = 64


def _ln_silu(c, g, b):
    mu = jnp.mean(c, axis=-1, keepdims=True)
    xc = c - mu
    var = jnp.mean(xc * xc, axis=-1, keepdims=True)
    y = xc * lax.rsqrt(var + EPS) * g + b
    return y * jax.nn.sigmoid(y)


def _conv_body(u_ref, halo_ref, left_ref, w_ref, bdw_ref, g_ref, b_ref, o_ref, ext_ref, y_ref, *, tt):
    first = pl.program_id(1) == 0
    ext_ref[0:CONV_HALO, :] = jnp.where(first, left_ref[...], halo_ref[...])
    ext_ref[CONV_HALO:, :] = u_ref[...]
    d_model = u_ref.shape[-1]
    lead = CONV_HALO - (CONV_WIDTH - 1)

    def chunk(c, carry):
        lanes = pl.ds(pl.multiple_of(c * V7X_LANES, V7X_LANES), V7X_LANES)
        w = w_ref[:, lanes]
        for r0 in range(0, tt, CONV_ROWS):
            n_win = CONV_ROWS + CONV_HALO
            win = ext_ref[r0:r0 + n_win, lanes]
            acc = jnp.zeros((CONV_ROWS, V7X_LANES), F32)
            for res in range(V7X_SUBLANES):
                shifted = win if res == 0 else pltpu.roll(win, n_win - res, axis=0)
                for off in range(res, lead + CONV_WIDTH, V7X_SUBLANES):
                    k = off - lead
                    if k >= 0:
                        acc = acc + shifted[off - res:off - res + CONV_ROWS, :] * w[k:k + 1, :]
            y_ref[r0:r0 + CONV_ROWS, lanes] = acc
        return carry

    lax.fori_loop(0, d_model // V7X_LANES, chunk, 0)
    c = y_ref[...] + bdw_ref[...]
    o_ref[...] = _ln_silu(c, g_ref[...], b_ref[...]).astype(o_ref.dtype)


def conv_ln_silu(u, left, w_dw, b_dw, ln_g, ln_b, *, tt):
    B, T, D = u.shape
    left_pad = jnp.pad(left, ((0, 0), (CONV_HALO - (CONV_WIDTH - 1), 0), (0, 0)))
    hb = tt // CONV_HALO
    row = lambda a: a.reshape(1, D)
    est = 2 * tt * D * 4 + 4 * CONV_HALO * D * 4 + (2 * tt + CONV_HALO) * D * 4 + 2 * tt * D * 2 + 6 * tt * D * 4
    return pl.pallas_call(
        functools.partial(_conv_body, tt=tt), grid=(B, T // tt),
        in_specs=[
            pl.BlockSpec((None, tt, D), lambda b, i: (b, i, 0)),
            pl.BlockSpec((None, CONV_HALO, D), lambda b, i: (b, jnp.maximum(i * hb - 1, 0), 0)),
            pl.BlockSpec((None, CONV_HALO, D), lambda b, i: (b, 0, 0)),
            pl.BlockSpec((CONV_WIDTH, D), lambda b, i: (0, 0)),
            pl.BlockSpec((1, D), lambda b, i: (0, 0)),
            pl.BlockSpec((1, D), lambda b, i: (0, 0)),
            pl.BlockSpec((1, D), lambda b, i: (0, 0)),
        ],
        out_specs=pl.BlockSpec((None, tt, D), lambda b, i: (b, i, 0)),
        out_shape=jax.ShapeDtypeStruct((B, T, D), BF16),
        scratch_shapes=[pltpu.VMEM((tt + CONV_HALO, D), F32), pltpu.VMEM((tt, D), F32)],
        name="conv_ln_silu",
        compiler_params=pltpu.CompilerParams(dimension_semantics=("arbitrary", "arbitrary"),
                                             vmem_limit_bytes=_vmem_limit(est)),
    )(u, u, left_pad, w_dw, row(b_dw), row(ln_g), row(ln_b))


def _conv_step_body(state_ref, u_ref, w_ref, bdw_ref, g_ref, b_ref, o_ref):
    acc = u_ref[...] * w_ref[CONV_WIDTH - 1:CONV_WIDTH, :]
    for k in range(CONV_WIDTH - 1):
        acc = acc + state_ref[:, k, :] * w_ref[k:k + 1, :]
    c = acc + bdw_ref[...]
    o_ref[...] = _ln_silu(c, g_ref[...], b_ref[...]).astype(o_ref.dtype)


def conv_step(state, u_new, w_dw, b_dw, ln_g, ln_b):
    NB, _, D = state.shape
    row = lambda a: a.reshape(1, D)
    return pl.pallas_call(
        _conv_step_body,
        out_shape=jax.ShapeDtypeStruct((NB, D), BF16), name="conv_step",
    )(state, u_new, w_dw, row(b_dw), row(ln_g), row(ln_b))


def _relbias_body(table_ref, o_ref):
    h = pl.program_id(0)
    a = lax.broadcasted_iota(jnp.int32, (MOBA_BLOCK, MOBA_BLOCK), 0)
    b = lax.broadcasted_iota(jnp.int32, (MOBA_BLOCK, MOBA_BLOCK), 1)
    far = table_ref[NUM_BUCKETS - 1, h]
    table_at = lambda k: table_ref[k, h]
    d_own = a - b
    own = (_bias_from_distance(d_own, table_at) - far) * LOG2E
    o_ref[0] = jnp.where(d_own >= 0, own, NEG)
    o_ref[1] = (_bias_from_distance(d_own + MOBA_BLOCK, table_at) - far) * LOG2E


def relbias_tables(rel_bias):
    return pl.pallas_call(
        _relbias_body, grid=(N_HEADS,),
        in_specs=[pl.BlockSpec(memory_space=pltpu.SMEM)],
        out_specs=pl.BlockSpec((None, 2, MOBA_BLOCK, MOBA_BLOCK), lambda h: (h, 0, 0, 0)),
        out_shape=jax.ShapeDtypeStruct((N_HEADS, 2, MOBA_BLOCK, MOBA_BLOCK), F32), name="relbias_tables",
    )(rel_bias)


GATE_ROWS = 16


def _moba_prompt_body(q_ref, k_ref, v_ref, bias_ref, o_ref, kx_ref, vx_ref, s_ref, p_ref, acc_ref, *, nb):
    blk = MOBA_BLOCK
    hd = q_ref.shape[-1]
    seq = nb * blk
    nt = (((1,), (1,)), ((), ()))

    k = k_ref[...]
    kx_ref[:, 0:hd] = k
    kr = lax.broadcasted_iota(jnp.int32, (seq, V7X_LANES), 0)
    kc = lax.broadcasted_iota(jnp.int32, (seq, V7X_LANES), 1)
    kx_ref[:, hd:] = (kr // blk == kc).astype(BF16)
    vx_ref[:, 0:hd] = v_ref[...]
    vx_ref[:, hd:] = (kc == 0).astype(BF16)
    r = lax.broadcasted_iota(jnp.int32, (GATE_ROWS, seq), 0)
    c = lax.broadcasted_iota(jnp.int32, (GATE_ROWS, seq), 1)
    avg = jnp.where(c // blk == r, 1.0 / blk, 0.0).astype(BF16)
    km = jnp.dot(avg, k, preferred_element_type=F32)
    km_hi = km.astype(BF16)
    km_lo = (km - km_hi.astype(F32)).astype(BF16)

    q = q_ref[...]
    g2 = (lax.dot_general(km_hi, q, nt, preferred_element_type=F32)
          + lax.dot_general(km_lo, q, nt, preferred_element_type=F32))
    gate = g2[0:V7X_SUBLANES]
    row = lax.broadcasted_iota(jnp.int32, gate.shape, 0)
    own_blk = lax.broadcasted_iota(jnp.int32, gate.shape, 1) // blk
    gate = jnp.where(row < own_blk, gate, NEG)
    beaten = jnp.zeros(gate.shape, jnp.int32)
    for jp in range(nb):
        gj = gate[jp:jp + 1, :]
        beaten = beaten + ((gj > gate) | ((gj == gate) & (jp < row))).astype(jnp.int32)
    chosen = (beaten < MOBA_TOPK) & (row < own_blk)
    mask_t = jnp.where(chosen | (row == own_blk), 0.0, NEG)
    mask_t = jnp.concatenate([mask_t, jnp.zeros((V7X_LANES - V7X_SUBLANES, seq), F32)], axis=0)
    qx = jnp.concatenate([q, mask_t.T.astype(BF16)], axis=1)

    for j in range(nb):
        s_ref[j * blk:, j * blk:(j + 1) * blk] = lax.dot_general(
            qx[j * blk:, :], kx_ref[j * blk:(j + 1) * blk, :], nt, preferred_element_type=F32)
    for own in range(nb):
        n = (own + 1) * blk
        rows = slice(own * blk, n)
        parts = [s_ref[rows, n - blk:n] + bias_ref[0]]
        if own >= 1:
            parts.insert(0, s_ref[rows, n - 2 * blk:n - blk] + bias_ref[1])
        if own >= 2:
            parts.insert(0, s_ref[rows, 0:n - 2 * blk])
        s = jnp.concatenate(parts, axis=1) if len(parts) > 1 else parts[0]
        m = jnp.max(s, axis=-1, keepdims=True)
        p_ref[rows, 0:n] = jnp.exp2(s - m).astype(BF16)
    for j in range(nb):
        pv = jnp.dot(p_ref[j * blk:, j * blk:(j + 1) * blk], vx_ref[j * blk:(j + 1) * blk, :],
                     preferred_element_type=F32)
        if j == 0:
            acc_ref[...] = pv
        else:
            acc_ref[j * blk:, :] += pv
    acc = acc_ref[...]
    o_ref[...] = (acc[:, 0:hd] * (1.0 / acc[:, hd:hd + 1])).astype(o_ref.dtype)


def moba_prompt_attn(q, k, v, bias, *, n_seq, seq):
    M, HD = q.shape
    hd = HD // N_HEADS
    nb = seq // MOBA_BLOCK
    assert seq % MOBA_BLOCK == 0 and nb <= V7X_SUBLANES
    head_spec = pl.BlockSpec((seq, hd), lambda b, h: (b, h))
    return pl.pallas_call(
        functools.partial(_moba_prompt_body, nb=nb), grid=(n_seq, N_HEADS),
        in_specs=[head_spec, head_spec, head_spec,
                  pl.BlockSpec((None, 2, MOBA_BLOCK, MOBA_BLOCK), lambda b, h: (h, 0, 0, 0))],
        out_specs=head_spec,
        out_shape=jax.ShapeDtypeStruct((M, HD), BF16),
        scratch_shapes=[pltpu.VMEM((seq, hd + V7X_LANES), BF16), pltpu.VMEM((seq, hd + V7X_LANES), BF16),
                        pltpu.VMEM((seq, seq), F32), pltpu.VMEM((seq, seq), BF16),
                        pltpu.VMEM((seq, hd + V7X_LANES), F32)],
        name="moba_prompt_attn",
        compiler_params=pltpu.CompilerParams(dimension_semantics=("arbitrary", "arbitrary"),
                                             vmem_limit_bytes=_vmem_limit(seq * seq * 6 + 16 * seq * hd * 4)),
    )(q, k, v, bias)


def _gate_topk_body(q_ref, sums_ref, knew_ref, o_ref, *, nbp, ppb, own, inv_rows):
    b = pl.program_id(0)
    HD = q_ref.shape[-1]
    hd = HD // N_HEADS
    q = q_ref[pl.ds(b, 1), :]
    r = lax.broadcasted_iota(jnp.int32, (HD, V7X_LANES), 0)
    c = lax.broadcasted_iota(jnp.int32, (HD, V7X_LANES), 1)
    ind = (r // hd == c).astype(BF16)

    def head_sums(prod):
        hi = prod.astype(BF16)
        mid = (prod - hi.astype(F32)).astype(BF16)
        lo = (prod - hi.astype(F32) - mid.astype(F32)).astype(BF16)
        return (jnp.dot(hi, ind, preferred_element_type=F32) + jnp.dot(mid, ind, preferred_element_type=F32)
                + jnp.dot(lo, ind, preferred_element_type=F32))

    block_sums = sums_ref[:, 0:HD]
    for p in range(1, ppb):
        block_sums = block_sums + sums_ref[:, p * HD:(p + 1) * HD]
    means = block_sums * inv_rows
    gate_past = head_sums(means * q)
    k_new = jnp.broadcast_to(knew_ref[pl.ds(b, 1), :], (V7X_SUBLANES, HD))
    gate_rec = head_sums((k_new * inv_rows) * q)
    gate = jnp.concatenate([gate_past, gate_rec], axis=0)
    idx = lax.broadcasted_iota(jnp.int32, gate.shape, 0)
    gate = jnp.where(idx < own, gate, NEG)
    gate = jnp.where(idx <= nbp, gate, -jnp.inf)
    picks = []
    for _ in range(MOBA_TOPK):
        mx = jnp.max(gate, axis=0, keepdims=True)
        pick = jnp.min(jnp.where(gate == mx, idx, nbp + V7X_SUBLANES), axis=0, keepdims=True)
        picks.append(pick)
        gate = jnp.where(idx == pick, -jnp.inf, gate)
    picks.append(jnp.zeros((V7X_SUBLANES - MOBA_TOPK, V7X_LANES), jnp.int32))
    o_ref[...] = jnp.concatenate(picks, axis=0)


def moba_gate_topk(q_rows, page_sums, k_new_rows, *, n_seq, ppb):
    R, HD = q_rows.shape
    _, nbp, _ = page_sums.shape
    return pl.pallas_call(
        functools.partial(_gate_topk_body, nbp=nbp, ppb=ppb, own=nbp, inv_rows=1.0 / MOBA_BLOCK), grid=(n_seq,),
        in_specs=[
            pl.BlockSpec((R, HD), lambda b: (0, 0)),
            pl.BlockSpec((None, nbp, ppb * HD), lambda b: (b, 0, 0)),
            pl.BlockSpec((R, HD), lambda b: (0, 0)),
        ],
        out_specs=pl.BlockSpec((None, V7X_SUBLANES, V7X_LANES), lambda b: (b, 0, 0)),
        out_shape=jax.ShapeDtypeStruct((n_seq, V7X_SUBLANES, V7X_LANES), jnp.int32), name="moba_gate_topk",
        compiler_params=pltpu.CompilerParams(dimension_semantics=("arbitrary",)),
    )(q_rows, page_sums, k_new_rows)


DEC_HEADS_PER_STEP = 8


def _decode_attn_body(sel_ref, pt_ref, table_ref, q_ref, kn_ref, vn_ref, ck_ref, cv_ref, o_ref,
                      kbuf, vbuf, sem, *, nbp, ppb, t_new, scale):
    step = pl.program_id(0)
    hd = kbuf.shape[-1]
    hg = DEC_HEADS_PER_STEP
    groups = N_HEADS // hg
    n_buf = MOBA_TOPK * ppb

    def page_copies(st, slot):
        b = st // groups
        out = []
        for hh in range(hg):
            h = (st % groups) * hg + hh
            for kk in range(MOBA_TOPK):
                blk = jnp.minimum(sel_ref[b, h, kk], nbp - 1)
                for p in range(ppb):
                    page = pt_ref[b, blk * ppb + p]
                    i = hh * n_buf + kk * ppb + p
                    out.append(pltpu.make_async_copy(ck_ref.at[page, :, h, :], kbuf.at[slot, i], sem.at[0, slot, i]))
                    out.append(pltpu.make_async_copy(cv_ref.at[page, :, h, :], vbuf.at[slot, i], sem.at[1, slot, i]))
        return out

    slot = step % 2

    @pl.when(step == 0)
    def _():
        for c in page_copies(step, slot):
            c.start()

    @pl.when(step + 1 < pl.num_programs(0))
    def _():
        for c in page_copies(step + 1, 1 - slot):
            c.start()

    for c in page_copies(step, slot):
        c.wait()

    b = step // groups
    nt = (((1,), (1,)), ((), ()))
    for hh in range(hg):
        h = (step % groups) * hg + hh
        lanes = pl.ds(pl.multiple_of(h * hd, hd), hd)

        def rows8(ref):
            x = ref[:, lanes]
            r = lax.broadcasted_iota(jnp.int32, x.shape, 0)
            row = jnp.sum(jnp.where(r == b, x, 0.0), axis=0, keepdims=True)
            return jnp.broadcast_to(row, (V7X_SUBLANES, hd))

        q = rows8(q_ref)
        qb = q.astype(BF16)
        table_at = lambda k, h=h: table_ref[k, h]
        kn, vn = rows8(kn_ref), rows8(vn_ref)
        l_new = jnp.sum(q * kn, axis=-1, keepdims=True) * scale + table_at(0)
        logits, values = [], []
        for kk in range(MOBA_TOPK):
            sel = sel_ref[b, h, kk]
            valid = sel < nbp
            selp = jnp.minimum(sel, nbp - 1)
            for p in range(ppb):
                i = hh * n_buf + kk * ppb + p
                kp = kbuf[slot, i].astype(BF16)
                s = lax.dot_general(qb, kp, nt, preferred_element_type=F32) * scale
                pos = selp * MOBA_BLOCK + p * PAGE_SIZE + lax.broadcasted_iota(jnp.int32, s.shape, 1)
                s = s + _bias_from_distance(t_new - pos, table_at)
                logits.append(jnp.where(valid, s, NEG))
                values.append(vbuf[slot, i].astype(BF16))
        m = l_new
        for s in logits:
            m = jnp.maximum(m, jnp.max(s, axis=-1, keepdims=True))
        p_new = jnp.exp(l_new - m)
        den = p_new
        acc = p_new * vn
        for s, v in zip(logits, values):
            p = jnp.exp(s - m)
            den = den + jnp.sum(p, axis=-1, keepdims=True)
            acc = acc + jnp.dot(p.astype(BF16), v, preferred_element_type=F32)
        o_ref[hh] = acc / den


def moba_decode_attn(sel, page_table, rel_bias, q_rows, k_new_rows, v_new_rows, cache_k, cache_v, *, n_seq):
    R, HD = q_rows.shape
    hd = HD // N_HEADS
    n_pages = page_table.shape[1]
    ppb = MOBA_BLOCK // PAGE_SIZE
    nbp = n_pages // ppb
    hg = DEC_HEADS_PER_STEP
    assert n_pages % ppb == 0, "past length must be a whole number of MoBA blocks"
    assert N_HEADS % hg == 0
    row_spec = pl.BlockSpec((R, HD), lambda s, sel, pt: (0, 0))
    n_buf = hg * MOBA_TOPK * ppb
    body = functools.partial(_decode_attn_body, nbp=nbp, ppb=ppb, t_new=n_pages * PAGE_SIZE, scale=hd ** -0.5)
    return pl.pallas_call(
        body,
        grid_spec=pltpu.PrefetchScalarGridSpec(
            num_scalar_prefetch=2, grid=(n_seq * N_HEADS // hg,),
            in_specs=[pl.BlockSpec(memory_space=pltpu.SMEM), row_spec, row_spec, row_spec,
                      pl.BlockSpec(memory_space=pl.ANY), pl.BlockSpec(memory_space=pl.ANY)],
            out_specs=pl.BlockSpec((hg, V7X_SUBLANES, hd), lambda s, sel, pt: (s, 0, 0)),
            scratch_shapes=[pltpu.VMEM((2, n_buf, PAGE_SIZE, hd), F32), pltpu.VMEM((2, n_buf, PAGE_SIZE, hd), F32),
                            pltpu.SemaphoreType.DMA((2, 2, n_buf))],
        ),
        out_shape=jax.ShapeDtypeStruct((n_seq * N_HEADS, V7X_SUBLANES, hd), F32), name="moba_decode_attn",
        compiler_params=pltpu.CompilerParams(dimension_semantics=("arbitrary",)),
    )(sel, page_table, rel_bias, q_rows, k_new_rows, v_new_rows, cache_k, cache_v)


ROW_TILE = 1024
DEC_ROWS = 16
CONV_TIME_TILE = 512


def kernel(x_prompt, x_sample, state_conv, cache_k, cache_v, page_table, norm_mix_g, norm_ffn_g, a_w_in, a_b_in, a_w_dw, a_b_dw, a_ln_g, a_ln_b, a_w_out, a_b_out, kv_norm_g, w_k, w_v, b_w_q, b_w_o, rel_bias, f_w_gate, f_w_up, f_w_down, final_norm_g):
    B, S, D = x_prompt.shape
    DB, dec_seq, _ = x_sample.shape
    assert dec_seq == 1, "decode path handles one new token per sequence"
    n_a = a_w_in.shape[0]
    depth = norm_mix_g.shape[0]
    F = f_w_gate.shape[-1]
    hd = D // N_HEADS
    ppb = MOBA_BLOCK // PAGE_SIZE
    n_pages = page_table.shape[1]

    bf = lambda w: w.astype(BF16)
    a_w_in_b, a_w_out_b = bf(a_w_in), bf(a_w_out)
    w_k_b, w_v_b, w_q_b, w_o_b = bf(w_k)[None], bf(w_v)[None], bf(b_w_q), bf(b_w_o)
    w_gate_b, w_up_b, w_down_b = bf(f_w_gate), bf(f_w_up), bf(f_w_down)

    xp = x_prompt.reshape(B * S, D)
    xs = jnp.pad(x_sample.reshape(DB, D), ((0, DEC_ROWS - DB), (0, 0)))

    def mm(x, x_side, ws, n_cols, *, residual=None, outs=((F32, None),), side_dtypes=(F32,), out_scale=None,
           side_scale=None, **kw):
        res_main, res_side = residual if residual is not None else (None, None)
        return fused_matmul(x, ws, n_cols, residual=res_main, outs=outs, out_scale=out_scale,
                            side=(x_side, res_side, side_dtypes, side_scale), tm=ROW_TILE, **kw)

    conv_p, conv_s = [], []
    bias_tables = relbias_tables(rel_bias)
    page_ids = page_table.reshape(-1)
    ids_per_layer = -(-page_ids.shape[0] // n_a)
    page_sum_parts = []
    for l in range(depth):
        if l < n_a:
            b_in = a_b_in[l].reshape(1, 2 * D)
            u, u_s = mm(xp, xs, [(a_w_in_b, l, 0), (a_w_in_b, l, D)], D, gain=norm_mix_g[l],
                        biases=[(b_in, 0), (b_in, D)], epilogue="glu", name="glu")
            u3 = u.reshape(B, S, D)
            left = jnp.zeros((B, CONV_WIDTH - 1, D), F32)
            c = conv_ln_silu(u3, left, a_w_dw[l], a_b_dw[l], a_ln_g[l], a_ln_b[l],
                             tt=CONV_TIME_TILE).reshape(B * S, D)
            conv_p.append(u3[:, S - (CONV_WIDTH - 1):, :])
            c_s = conv_step(state_conv[l], u_s[:DB], a_w_dw[l], a_b_dw[l], a_ln_g[l], a_ln_b[l])
            c_s = jnp.pad(c_s, ((0, DEC_ROWS - DB), (0, 0)))
            conv_s.append(jnp.concatenate([state_conv[l][:, 1:, :], u_s[:DB, None, :]], axis=1))
            xp, xs = mm(c, c_s, [(a_w_out_b, l, 0)], D, biases=[(a_b_out[l].reshape(1, D), 0)],
                        residual=(xp, xs), name="convout")
        else:
            j = l - n_a
            if l == n_a:
                kv_outs = ((F32, hd), (BF16, None))
                k_heads, k_b, k_s = mm(xp, xs, [(w_k_b, 0, 0)], D, gain=kv_norm_g, outs=kv_outs, name="kproj")
                v_heads, v_b, v_s = mm(xp, xs, [(w_v_b, 0, 0)], D, gain=kv_norm_g, outs=kv_outs, name="vproj")
                page_sums = jnp.concatenate(page_sum_parts).reshape(DB, n_pages // ppb, ppb * D)
            q, q_s = mm(xp, xs, [(w_q_b, j, 0)], D, gain=norm_mix_g[l], outs=((BF16, None),),
                        out_scale=hd ** -0.5 * LOG2E, name="qproj")
            o = moba_prompt_attn(q, k_b, v_b, bias_tables, n_seq=B, seq=S)
            picks = moba_gate_topk(q_s, page_sums, k_s, n_seq=DB, ppb=ppb)
            sel = picks[:, :MOBA_TOPK, :N_HEADS].transpose(0, 2, 1)
            o8 = moba_decode_attn(sel, page_table, rel_bias, q_s, k_s, v_s, cache_k, cache_v, n_seq=DB)
            o_s = jnp.pad(o8[:, 0, :].reshape(DB, D), ((0, DEC_ROWS - DB), (0, 0))).astype(BF16)
            xp, xs = mm(o, o_s, [(w_o_b, j, 0)], D, residual=(xp, xs), name="oproj")
        ids = page_ids[l * ids_per_layer:(l + 1) * ids_per_layer] if l < n_a else None
        up = mm(xp, xs, [(w_gate_b, l, 0), (w_up_b, l, 0)], F, gain=norm_ffn_g[l], epilogue="swiglu",
                outs=((BF16, None),), side_dtypes=(BF16,),
                page_stream=None if ids is None else (cache_k, ids), name="ffn_up")
        act, act_s = up[0], up[1]
        if ids is not None:
            page_sum_parts.append(up[2][:ids.shape[0]])
        xp, xs = mm(act, act_s, [(w_down_b, l, 0)], D, residual=(xp, xs), name="ffn_down")

    y_prompt = rmsnorm_rows(xp, final_norm_g, tm=ROW_TILE // 2).reshape(B, S, D)
    y_sample = rmsnorm_rows(xs, final_norm_g, tm=DEC_ROWS)[:DB].reshape(DB, 1, D)
    return (y_prompt, y_sample, jnp.stack(conv_p), jnp.stack(conv_s),
            k_heads.reshape(B, S, N_HEADS, hd), v_heads.reshape(B, S, N_HEADS, hd),
            k_s[:DB].reshape(DB, 1, N_HEADS, hd), v_s[:DB].reshape(DB, 1, N_HEADS, hd))
```

```python
import functools
import math

import numpy as np
import jax
import jax.numpy as jnp
from jax import lax
from jax.experimental import pallas as pl
from jax.experimental.pallas import tpu as pltpu

F32 = jnp.float32
BF16 = jnp.bfloat16

N_HEADS = 16
CONV_WIDTH = 31
MOBA_BLOCK = 256
MOBA_TOPK = 3
NUM_BUCKETS = 32
MAX_DISTANCE = 128
PAGE_SIZE = 128
EPS = 1e-6
NEG = -1e30
LOG2E = math.log2(math.e)

V7X_VMEM_BYTES = 64 * 1024 * 1024
V7X_LANES = 128
V7X_SUBLANES = 8
VMEM_CAP = V7X_VMEM_BYTES - 4 * 1024 * 1024


def _vmem_limit(est_bytes):
    return int(min(max(est_bytes + (8 << 20), 16 << 20), VMEM_CAP))


def _bucket_thresholds():
    max_exact = NUM_BUCKETS // 2
    n = np.arange(0, 4 * MAX_DISTANCE, dtype=np.int64)
    nf = np.maximum(n, max_exact).astype(np.float32)
    large = max_exact + (np.log(nf / np.float32(max_exact)) / np.float32(math.log(MAX_DISTANCE / max_exact))
                         * np.float32(NUM_BUCKETS - max_exact)).astype(np.int32)
    large = np.minimum(large, NUM_BUCKETS - 1)
    bucket = np.where(n < max_exact, n, large)
    assert np.all(np.diff(bucket) >= 0) and bucket[-1] == NUM_BUCKETS - 1
    return [int(np.argmax(bucket >= k)) for k in range(1, NUM_BUCKETS)]


_BUCKET_START = _bucket_thresholds()


def _bias_from_distance(dist, table_at):
    bias = jnp.full(dist.shape, table_at(0), F32)
    for k in range(1, NUM_BUCKETS):
        bias = jnp.where(dist >= _BUCKET_START[k - 1], table_at(k), bias)
    return bias


def _mm_body(*refs, has_norm, n_w, has_bias, has_res, has_side, epilogue, out_scales, n_outs, n_pages):
    it = iter(refs)
    if n_pages:
        next(it)
    x_refs = [next(it) for _ in range(1 + has_side)]
    g_ref = next(it) if has_norm else None
    w_refs = [next(it) for _ in range(n_w)]
    b_refs = [next(it) for _ in range(n_w)] if has_bias else []
    res_refs = [next(it) for _ in range(1 + has_side)] if has_res else [None, None]
    page_refs = [next(it) for _ in range(n_pages)]
    out_refs = [[next(it) for _ in range(n)] for n in n_outs]
    page_out = next(it) if n_pages else None
    if has_norm:
        xn_refs = [next(it) for _ in x_refs]

        @pl.when(pl.program_id(1) == 0)
        def _():
            for x_ref, xn_ref in zip(x_refs, xn_refs):
                x = x_ref[...]
                ms = jnp.mean(x * x, axis=-1, keepdims=True)
                xn_ref[...] = (x * lax.rsqrt(ms + EPS) * g_ref[...]).astype(xn_ref.dtype)

        x_refs = xn_refs

    def stream(x_ref, res_ref, outs, out_scale):
        xb = x_ref[...]
        accs = [jnp.dot(xb, w[...], preferred_element_type=F32) for w in w_refs]
        if has_bias:
            accs = [a + b[...] for a, b in zip(accs, b_refs)]
        if epilogue == "glu":
            y = accs[0] * jax.nn.sigmoid(accs[1])
        elif epilogue == "swiglu":
            y = jax.nn.silu(accs[0]) * accs[1]
        else:
            y = accs[0]
        if out_scale is not None:
            y = y * out_scale
        if has_res:
            y = res_ref[...] + y
        for o in outs:
            if len(o.shape) == 3:
                hd = o.shape[-1]
                for h in range(o.shape[1]):
                    o[:, h, :] = y[:, h * hd:(h + 1) * hd].astype(o.dtype)
            else:
                o[...] = y.astype(o.dtype)

    stream(x_refs[0], res_refs[0], out_refs[0], out_scales[0])
    if has_side:
        @pl.when(pl.program_id(0) == 0)
        def _():
            stream(x_refs[1], res_refs[1], out_refs[1], out_scales[1])

    for k, page_ref in enumerate(page_refs):
        page_out[k] = jnp.sum(page_ref[...], axis=0)


MATMUL_VMEM_BUDGET = 48 * 1024 * 1024


def fused_matmul(x, ws, n_cols, *, gain=None, biases=None, residual=None, epilogue="none",
                 outs=((F32, None),), out_scale=None, side=None, page_stream=None, tm, name):
    M, K = x.shape
    has_norm = gain is not None
    has_side = side is not None
    out_bytes = sum(jnp.dtype(dt).itemsize for dt, _ in outs) + (4 if residual is not None else 0)

    def pages_per_step(tn):
        return -(-page_stream[1].shape[0] // ((M // tm) * (n_cols // tn))) if page_stream is not None else 0

    def estimate(tn):
        page_bytes = 2 * pages_per_step(tn) * math.prod(page_stream[0].shape[1:]) * 4 if page_stream else 0
        return (2 * tm * K * x.dtype.itemsize + (tm * K * 2 if has_norm else 0) + 2 * len(ws) * K * tn * 2
                + 2 * tm * tn * out_bytes + (len(ws) + 1) * tm * tn * 4 + page_bytes)

    def legal(tn):
        heads_ok = all(hd is None or (tn // hd) % V7X_SUBLANES == 0 or tn == n_cols for _, hd in outs)
        offs_ok = all(off % tn == 0 for _, _, off in ws)
        return n_cols % tn == 0 and heads_ok and offs_ok

    tiles = [t for t in (2048, 1024, 512, 256, 128) if legal(t)]
    fitting = [t for t in tiles if estimate(t) <= MATMUL_VMEM_BUDGET]
    tn = fitting[0] if fitting else tiles[-1]
    assert M % tm == 0
    nj = n_cols // tn
    grid = (M // tm, nj)
    in_specs = [pl.BlockSpec((tm, K), lambda i, j, *_: (i, 0))]
    args = [x]
    if has_side:
        x_side, res_side, side_dtypes, side_scale = side
        R = x_side.shape[0]
        side_block = lambda i, j, *_: (0, jnp.where(i == 0, j, nj - 1))
        in_specs.append(pl.BlockSpec((R, K), lambda i, j, *_: (0, 0)))
        args.append(x_side)
    if has_norm:
        in_specs.append(pl.BlockSpec((1, K), lambda i, j, *_: (0, 0)))
        args.append(gain.reshape(1, K))
    for w, layer, off in ws:
        in_specs.append(pl.BlockSpec((None, K, tn), lambda i, j, *_, layer=layer, ob=off // tn: (layer, 0, j + ob)))
        args.append(w)
    if biases is not None:
        for b, off in biases:
            in_specs.append(pl.BlockSpec((1, tn), lambda i, j, *_, ob=off // tn: (0, j + ob)))
            args.append(b)
    if residual is not None:
        in_specs.append(pl.BlockSpec((tm, tn), lambda i, j, *_: (i, j)))
        args.append(residual)
        if has_side:
            in_specs.append(pl.BlockSpec((R, tn), side_block))
            args.append(res_side)
    out_specs, out_shape = [], []
    for dt, hd in outs:
        if hd is None:
            out_specs.append(pl.BlockSpec((tm, tn), lambda i, j, *_: (i, j)))
            out_shape.append(jax.ShapeDtypeStruct((M, n_cols), dt))
        else:
            out_specs.append(pl.BlockSpec((tm, tn // hd, hd), lambda i, j, *_: (i, j, 0)))
            out_shape.append(jax.ShapeDtypeStruct((M, n_cols // hd, hd), dt))
    n_outs, out_scales = [len(outs)], [out_scale]
    if has_side:
        for dt in side_dtypes:
            out_specs.append(pl.BlockSpec((R, tn), side_block))
            out_shape.append(jax.ShapeDtypeStruct((R, n_cols), dt))
        n_outs.append(len(side_dtypes))
        out_scales.append(side_scale)
    prefetch = []
    per_step = pages_per_step(tn)
    if page_stream is not None:
        cache, ids = page_stream
        n_slots = grid[0] * nj * per_step
        assert ids.shape[0] <= n_slots
        prefetch = [jnp.concatenate([ids, jnp.broadcast_to(ids[-1:], (n_slots - ids.shape[0],))])]
        for k in range(per_step):
            in_specs.append(pl.BlockSpec(
                (None,) + cache.shape[1:],
                lambda i, j, ids_ref, k=k: (ids_ref[(i * nj + j) * per_step + k], 0, 0, 0)))
            args.append(cache)
        out_specs.append(pl.BlockSpec((per_step,) + cache.shape[2:], lambda i, j, *_: (i * nj + j, 0, 0)))
        out_shape.append(jax.ShapeDtypeStruct((n_slots,) + cache.shape[2:], F32))
    scratch = []
    if has_norm:
        scratch = [pltpu.VMEM((tm, K), BF16)] + ([pltpu.VMEM((R, K), BF16)] if has_side else [])
    body = functools.partial(_mm_body, has_norm=has_norm, n_w=len(ws), has_bias=biases is not None,
                             has_res=residual is not None, has_side=has_side, epilogue=epilogue,
                             out_scales=out_scales, n_outs=n_outs, n_pages=per_step)
    res = pl.pallas_call(
        body,
        grid_spec=pltpu.PrefetchScalarGridSpec(num_scalar_prefetch=len(prefetch), grid=grid, in_specs=in_specs,
                                               out_specs=out_specs, scratch_shapes=scratch),
        out_shape=out_shape, name=name,
        compiler_params=pltpu.CompilerParams(dimension_semantics=("arbitrary", "arbitrary"),
                                             vmem_limit_bytes=_vmem_limit(estimate(tn))),
    )(*prefetch, *args)
    return res if len(res) > 1 else res[0]


def _rmsnorm_body(x_ref, g_ref, o_ref):
    x = x_ref[...]
    ms = jnp.mean(x * x, axis=-1, keepdims=True)
    o_ref[...] = x * lax.rsqrt(ms + EPS) * g_ref[...]


def rmsnorm_rows(x, gain, *, tm):
    M, D = x.shape
    return pl.pallas_call(
        _rmsnorm_body, grid=(M // tm,),
        in_specs=[pl.BlockSpec((tm, D), lambda i: (i, 0)), pl.BlockSpec((1, D), lambda i: (0, 0))],
        out_specs=pl.BlockSpec((tm, D), lambda i: (i, 0)),
        out_shape=jax.ShapeDtypeStruct((M, D), F32), name="rmsnorm_rows",
        compiler_params=pltpu.CompilerParams(dimension_semantics=("arbitrary",),
                                             vmem_limit_bytes=_vmem_limit(4 * tm * D * 4)),
    )(x, gain.reshape(1, D))


CONV_HALO = 32
CONV_ROWS = 64


def _ln_silu(c, g, b):
    mu = jnp.mean(c, axis=-1, keepdims=True)
    xc = c - mu
    var = jnp.mean(xc * xc, axis=-1, keepdims=True)
    y = xc * lax.rsqrt(var + EPS) * g + b
    return y * jax.nn.sigmoid(y)


def _conv_body(u_ref, halo_ref, left_ref, w_ref, bdw_ref, g_ref, b_ref, o_ref, ext_ref, y_ref, *, tt):
    first = pl.program_id(1) == 0
    ext_ref[0:CONV_HALO, :] = jnp.where(first, left_ref[...], halo_ref[...])
    ext_ref[CONV_HALO:, :] = u_ref[...]
    d_model = u_ref.shape[-1]
    lead = CONV_HALO - (CONV_WIDTH - 1)

    def chunk(c, carry):
        lanes = pl.ds(pl.multiple_of(c * V7X_LANES, V7X_LANES), V7X_LANES)
        w = w_ref[:, lanes]
        for r0 in range(0, tt, CONV_ROWS):
            n_win = CONV_ROWS + CONV_HALO
            win = ext_ref[r0:r0 + n_win, lanes]
            acc = jnp.zeros((CONV_ROWS, V7X_LANES), F32)
            for res in range(V7X_SUBLANES):
                shifted = win if res == 0 else pltpu.roll(win, n_win - res, axis=0)
                for off in range(res, lead + CONV_WIDTH, V7X_SUBLANES):
                    k = off - lead
                    if k >= 0:
                        acc = acc + shifted[off - res:off - res + CONV_ROWS, :] * w[k:k + 1, :]
            y_ref[r0:r0 + CONV_ROWS, lanes] = acc
        return carry

    lax.fori_loop(0, d_model // V7X_LANES, chunk, 0)
    c = y_ref[...] + bdw_ref[...]
    o_ref[...] = _ln_silu(c, g_ref[...], b_ref[...]).astype(o_ref.dtype)


def conv_ln_silu(u, left, w_dw, b_dw, ln_g, ln_b, *, tt):
    B, T, D = u.shape
    left_pad = jnp.pad(left, ((0, 0), (CONV_HALO - (CONV_WIDTH - 1), 0), (0, 0)))
    hb = tt // CONV_HALO
    row = lambda a: a.reshape(1, D)
    est = 2 * tt * D * 4 + 4 * CONV_HALO * D * 4 + (2 * tt + CONV_HALO) * D * 4 + 2 * tt * D * 2 + 6 * tt * D * 4
    return pl.pallas_call(
        functools.partial(_conv_body, tt=tt), grid=(B, T // tt),
        in_specs=[
            pl.BlockSpec((None, tt, D), lambda b, i: (b, i, 0)),
            pl.BlockSpec((None, CONV_HALO, D), lambda b, i: (b, jnp.maximum(i * hb - 1, 0), 0)),
            pl.BlockSpec((None, CONV_HALO, D), lambda b, i: (b, 0, 0)),
            pl.BlockSpec((CONV_WIDTH, D), lambda b, i: (0, 0)),
            pl.BlockSpec((1, D), lambda b, i: (0, 0)),
            pl.BlockSpec((1, D), lambda b, i: (0, 0)),
            pl.BlockSpec((1, D), lambda b, i: (0, 0)),
        ],
        out_specs=pl.BlockSpec((None, tt, D), lambda b, i: (b, i, 0)),
        out_shape=jax.ShapeDtypeStruct((B, T, D), BF16),
        scratch_shapes=[pltpu.VMEM((tt + CONV_HALO, D), F32), pltpu.VMEM((tt, D), F32)],
        name="conv_ln_silu",
        compiler_params=pltpu.CompilerParams(dimension_semantics=("arbitrary", "arbitrary"),
                                             vmem_limit_bytes=_vmem_limit(est)),
    )(u, u, left_pad, w_dw, row(b_dw), row(ln_g), row(ln_b))


def _conv_step_body(state_ref, u_ref, w_ref, bdw_ref, g_ref, b_ref, o_ref):
    acc = u_ref[...] * w_ref[CONV_WIDTH - 1:CONV_WIDTH, :]
    for k in range(CONV_WIDTH - 1):
        acc = acc + state_ref[:, k, :] * w_ref[k:k + 1, :]
    c = acc + bdw_ref[...]
    o_ref[...] = _ln_silu(c, g_ref[...], b_ref[...]).astype(o_ref.dtype)


def conv_step(state, u_new, w_dw, b_dw, ln_g, ln_b):
    NB, _, D = state.shape
    row = lambda a: a.reshape(1, D)
    return pl.pallas_call(
        _conv_step_body,
        out_shape=jax.ShapeDtypeStruct((NB, D), BF16), name="conv_step",
    )(state, u_new, w_dw, row(b_dw), row(ln_g), row(ln_b))


def _relbias_body(table_ref, o_ref):
    h = pl.program_id(0)
    a = lax.broadcasted_iota(jnp.int32, (MOBA_BLOCK, MOBA_BLOCK), 0)
    b = lax.broadcasted_iota(jnp.int32, (MOBA_BLOCK, MOBA_BLOCK), 1)
    far = table_ref[NUM_BUCKETS - 1, h]
    table_at = lambda k: table_ref[k, h]
    d_own = a - b
    own = (_bias_from_distance(d_own, table_at) - far) * LOG2E
    o_ref[0] = jnp.where(d_own >= 0, own, NEG)
    o_ref[1] = (_bias_from_distance(d_own + MOBA_BLOCK, table_at) - far) * LOG2E


def relbias_tables(rel_bias):
    return pl.pallas_call(
        _relbias_body, grid=(N_HEADS,),
        in_specs=[pl.BlockSpec(memory_space=pltpu.SMEM)],
        out_specs=pl.BlockSpec((None, 2, MOBA_BLOCK, MOBA_BLOCK), lambda h: (h, 0, 0, 0)),
        out_shape=jax.ShapeDtypeStruct((N_HEADS, 2, MOBA_BLOCK, MOBA_BLOCK), F32), name="relbias_tables",
    )(rel_bias)


GATE_ROWS = 16


def _moba_prompt_body(q_ref, k_ref, v_ref, bias_ref, o_ref, kx_ref, vx_ref, s_ref, p_ref, acc_ref, *, nb):
    blk = MOBA_BLOCK
    hd = q_ref.shape[-1]
    seq = nb * blk
    nt = (((1,), (1,)), ((), ()))

    k = k_ref[...]
    kx_ref[:, 0:hd] = k
    kr = lax.broadcasted_iota(jnp.int32, (seq, V7X_LANES), 0)
    kc = lax.broadcasted_iota(jnp.int32, (seq, V7X_LANES), 1)
    kx_ref[:, hd:] = (kr // blk == kc).astype(BF16)
    vx_ref[:, 0:hd] = v_ref[...]
    vx_ref[:, hd:] = (kc == 0).astype(BF16)
    r = lax.broadcasted_iota(jnp.int32, (GATE_ROWS, seq), 0)
    c = lax.broadcasted_iota(jnp.int32, (GATE_ROWS, seq), 1)
    avg = jnp.where(c // blk == r, 1.0 / blk, 0.0).astype(BF16)
    km = jnp.dot(avg, k, preferred_element_type=F32)
    km_hi = km.astype(BF16)
    km_lo = (km - km_hi.astype(F32)).astype(BF16)

    q = q_ref[...]
    g2 = (lax.dot_general(km_hi, q, nt, preferred_element_type=F32)
          + lax.dot_general(km_lo, q, nt, preferred_element_type=F32))
    gate = g2[0:V7X_SUBLANES]
    row = lax.broadcasted_iota(jnp.int32, gate.shape, 0)
    own_blk = lax.broadcasted_iota(jnp.int32, gate.shape, 1) // blk
    gate = jnp.where(row < own_blk, gate, NEG)
    beaten = jnp.zeros(gate.shape, jnp.int32)
    for jp in range(nb):
        gj = gate[jp:jp + 1, :]
        beaten = beaten + ((gj > gate) | ((gj == gate) & (jp < row))).astype(jnp.int32)
    chosen = (beaten < MOBA_TOPK) & (row < own_blk)
    mask_t = jnp.where(chosen | (row == own_blk), 0.0, NEG)
    mask_t = jnp.concatenate([mask_t, jnp.zeros((V7X_LANES - V7X_SUBLANES, seq), F32)], axis=0)
    qx = jnp.concatenate([q, mask_t.T.astype(BF16)], axis=1)

    for j in range(nb):
        s_ref[j * blk:, j * blk:(j + 1) * blk] = lax.dot_general(
            qx[j * blk:, :], kx_ref[j * blk:(j + 1) * blk, :], nt, preferred_element_type=F32)
    for own in range(nb):
        n = (own + 1) * blk
        rows = slice(own * blk, n)
        parts = [s_ref[rows, n - blk:n] + bias_ref[0]]
        if own >= 1:
            parts.insert(0, s_ref[rows, n - 2 * blk:n - blk] + bias_ref[1])
        if own >= 2:
            parts.insert(0, s_ref[rows, 0:n - 2 * blk])
        s = jnp.concatenate(parts, axis=1) if len(parts) > 1 else parts[0]
        m = jnp.max(s, axis=-1, keepdims=True)
        p_ref[rows, 0:n] = jnp.exp2(s - m).astype(BF16)
    for j in range(nb):
        pv = jnp.dot(p_ref[j * blk:, j * blk:(j + 1) * blk], vx_ref[j * blk:(j + 1) * blk, :],
                     preferred_element_type=F32)
        if j == 0:
            acc_ref[...] = pv
        else:
            acc_ref[j * blk:, :] += pv
    acc = acc_ref[...]
    o_ref[...] = (acc[:, 0:hd] * (1.0 / acc[:, hd:hd + 1])).astype(o_ref.dtype)


def moba_prompt_attn(q, k, v, bias, *, n_seq, seq):
    M, HD = q.shape
    hd = HD // N_HEADS
    nb = seq // MOBA_BLOCK
    assert seq % MOBA_BLOCK == 0 and nb <= V7X_SUBLANES
    head_spec = pl.BlockSpec((seq, hd), lambda b, h: (b, h))
    return pl.pallas_call(
        functools.partial(_moba_prompt_body, nb=nb), grid=(n_seq, N_HEADS),
        in_specs=[head_spec, head_spec, head_spec,
                  pl.BlockSpec((None, 2, MOBA_BLOCK, MOBA_BLOCK), lambda b, h: (h, 0, 0, 0))],
        out_specs=head_spec,
        out_shape=jax.ShapeDtypeStruct((M, HD), BF16),
        scratch_shapes=[pltpu.VMEM((seq, hd + V7X_LANES), BF16), pltpu.VMEM((seq, hd + V7X_LANES), BF16),
                        pltpu.VMEM((seq, seq), F32), pltpu.VMEM((seq, seq), BF16),
                        pltpu.VMEM((seq, hd + V7X_LANES), F32)],
        name="moba_prompt_attn",
        compiler_params=pltpu.CompilerParams(dimension_semantics=("arbitrary", "arbitrary"),
                                             vmem_limit_bytes=_vmem_limit(seq * seq * 6 + 16 * seq * hd * 4)),
    )(q, k, v, bias)


def _gate_topk_body(q_ref, sums_ref, knew_ref, o_ref, *, nbp, ppb, own, inv_rows):
    b = pl.program_id(0)
    HD = q_ref.shape[-1]
    hd = HD // N_HEADS
    q = q_ref[pl.ds(b, 1), :]
    r = lax.broadcasted_iota(jnp.int32, (HD, V7X_LANES), 0)
    c = lax.broadcasted_iota(jnp.int32, (HD, V7X_LANES), 1)
    ind = (r // hd == c).astype(BF16)

    def head_sums(prod):
        hi = prod.astype(BF16)
        mid = (prod - hi.astype(F32)).astype(BF16)
        lo = (prod - hi.astype(F32) - mid.astype(F32)).astype(BF16)
        return (jnp.dot(hi, ind, preferred_element_type=F32) + jnp.dot(mid, ind, preferred_element_type=F32)
                + jnp.dot(lo, ind, preferred_element_type=F32))

    block_sums = sums_ref[:, 0:HD]
    for p in range(1, ppb):
        block_sums = block_sums + sums_ref[:, p * HD:(p + 1) * HD]
    means = block_sums * inv_rows
    gate_past = head_sums(means * q)
    k_new = jnp.broadcast_to(knew_ref[pl.ds(b, 1), :], (V7X_SUBLANES, HD))
    gate_rec = head_sums((k_new * inv_rows) * q)
    gate = jnp.concatenate([gate_past, gate_rec], axis=0)
    idx = lax.broadcasted_iota(jnp.int32, gate.shape, 0)
    gate = jnp.where(idx < own, gate, NEG)
    gate = jnp.where(idx <= nbp, gate, -jnp.inf)
    picks = []
    for _ in range(MOBA_TOPK):
        mx = jnp.max(gate, axis=0, keepdims=True)
        pick = jnp.min(jnp.where(gate == mx, idx, nbp + V7X_SUBLANES), axis=0, keepdims=True)
        picks.append(pick)
        gate = jnp.where(idx == pick, -jnp.inf, gate)
    picks.append(jnp.zeros((V7X_SUBLANES - MOBA_TOPK, V7X_LANES), jnp.int32))
    o_ref[...] = jnp.concatenate(picks, axis=0)


def moba_gate_topk(q_rows, page_sums, k_new_rows, *, n_seq, ppb):
    R, HD = q_rows.shape
    _, nbp, _ = page_sums.shape
    return pl.pallas_call(
        functools.partial(_gate_topk_body, nbp=nbp, ppb=ppb, own=nbp, inv_rows=1.0 / MOBA_BLOCK), grid=(n_seq,),
        in_specs=[
            pl.BlockSpec((R, HD), lambda b: (0, 0)),
            pl.BlockSpec((None, nbp, ppb * HD), lambda b: (b, 0, 0)),
            pl.BlockSpec((R, HD), lambda b: (0, 0)),
        ],
        out_specs=pl.BlockSpec((None, V7X_SUBLANES, V7X_LANES), lambda b: (b, 0, 0)),
        out_shape=jax.ShapeDtypeStruct((n_seq, V7X_SUBLANES, V7X_LANES), jnp.int32), name="moba_gate_topk",
        compiler_params=pltpu.CompilerParams(dimension_semantics=("arbitrary",)),
    )(q_rows, page_sums, k_new_rows)


DEC_HEADS_PER_STEP = 8


def _decode_attn_body(sel_ref, pt_ref, table_ref, q_ref, kn_ref, vn_ref, ck_ref, cv_ref, o_ref,
                      kbuf, vbuf, sem, *, nbp, ppb, t_new, scale):
    step = pl.program_id(0)
    hd = kbuf.shape[-1]
    hg = DEC_HEADS_PER_STEP
    groups = N_HEADS // hg
    n_buf = MOBA_TOPK * ppb

    def page_copies(st, slot):
        b = st // groups
        out = []
        for hh in range(hg):
            h = (st % groups) * hg + hh
            for kk in range(MOBA_TOPK):
                blk = jnp.minimum(sel_ref[b, h, kk], nbp - 1)
                for p in range(ppb):
                    page = pt_ref[b, blk * ppb + p]
                    i = hh * n_buf + kk * ppb + p
                    out.append(pltpu.make_async_copy(ck_ref.at[page, :, h, :], kbuf.at[slot, i], sem.at[0, slot, i]))
                    out.append(pltpu.make_async_copy(cv_ref.at[page, :, h, :], vbuf.at[slot, i], sem.at[1, slot, i]))
        return out

    slot = step % 2

    @pl.when(step == 0)
    def _():
        for c in page_copies(step, slot):
            c.start()

    @pl.when(step + 1 < pl.num_programs(0))
    def _():
        for c in page_copies(step + 1, 1 - slot):
            c.start()

    for c in page_copies(step, slot):
        c.wait()

    b = step // groups
    nt = (((1,), (1,)), ((), ()))
    for hh in range(hg):
        h = (step % groups) * hg + hh
        lanes = pl.ds(pl.multiple_of(h * hd, hd), hd)

        def rows8(ref):
            x = ref[:, lanes]
            r = lax.broadcasted_iota(jnp.int32, x.shape, 0)
            row = jnp.sum(jnp.where(r == b, x, 0.0), axis=0, keepdims=True)
            return jnp.broadcast_to(row, (V7X_SUBLANES, hd))

        q = rows8(q_ref)
        qb = q.astype(BF16)
        table_at = lambda k, h=h: table_ref[k, h]
        kn, vn = rows8(kn_ref), rows8(vn_ref)
        l_new = jnp.sum(q * kn, axis=-1, keepdims=True) * scale + table_at(0)
        logits, values = [], []
        for kk in range(MOBA_TOPK):
            sel = sel_ref[b, h, kk]
            valid = sel < nbp
            selp = jnp.minimum(sel, nbp - 1)
            for p in range(ppb):
                i = hh * n_buf + kk * ppb + p
                kp = kbuf[slot, i].astype(BF16)
                s = lax.dot_general(qb, kp, nt, preferred_element_type=F32) * scale
                pos = selp * MOBA_BLOCK + p * PAGE_SIZE + lax.broadcasted_iota(jnp.int32, s.shape, 1)
                s = s + _bias_from_distance(t_new - pos, table_at)
                logits.append(jnp.where(valid, s, NEG))
                values.append(vbuf[slot, i].astype(BF16))
        m = l_new
        for s in logits:
            m = jnp.maximum(m, jnp.max(s, axis=-1, keepdims=True))
        p_new = jnp.exp(l_new - m)
        den = p_new
        acc = p_new * vn
        for s, v in zip(logits, values):
            p = jnp.exp(s - m)
            den = den + jnp.sum(p, axis=-1, keepdims=True)
            acc = acc + jnp.dot(p.astype(BF16), v, preferred_element_type=F32)
        o_ref[hh] = acc / den


def moba_decode_attn(sel, page_table, rel_bias, q_rows, k_new_rows, v_new_rows, cache_k, cache_v, *, n_seq):
    R, HD = q_rows.shape
    hd = HD // N_HEADS
    n_pages = page_table.shape[1]
    ppb = MOBA_BLOCK // PAGE_SIZE
    nbp = n_pages // ppb
    hg = DEC_HEADS_PER_STEP
    assert n_pages % ppb == 0, "past length must be a whole number of MoBA blocks"
    assert N_HEADS % hg == 0
    row_spec = pl.BlockSpec((R, HD), lambda s, sel, pt: (0, 0))
    n_buf = hg * MOBA_TOPK * ppb
    body = functools.partial(_decode_attn_body, nbp=nbp, ppb=ppb, t_new=n_pages * PAGE_SIZE, scale=hd ** -0.5)
    return pl.pallas_call(
        body,
        grid_spec=pltpu.PrefetchScalarGridSpec(
            num_scalar_prefetch=2, grid=(n_seq * N_HEADS // hg,),
            in_specs=[pl.BlockSpec(memory_space=pltpu.SMEM), row_spec, row_spec, row_spec,
                      pl.BlockSpec(memory_space=pl.ANY), pl.BlockSpec(memory_space=pl.ANY)],
            out_specs=pl.BlockSpec((hg, V7X_SUBLANES, hd), lambda s, sel, pt: (s, 0, 0)),
            scratch_shapes=[pltpu.VMEM((2, n_buf, PAGE_SIZE, hd), F32), pltpu.VMEM((2, n_buf, PAGE_SIZE, hd), F32),
                            pltpu.SemaphoreType.DMA((2, 2, n_buf))],
        ),
        out_shape=jax.ShapeDtypeStruct((n_seq * N_HEADS, V7X_SUBLANES, hd), F32), name="moba_decode_attn",
        compiler_params=pltpu.CompilerParams(dimension_semantics=("arbitrary",)),
    )(sel, page_table, rel_bias, q_rows, k_new_rows, v_new_rows, cache_k, cache_v)


ROW_TILE = 1024
DEC_ROWS = 16
CONV_TIME_TILE = 512


def kernel(x_prompt, x_sample, state_conv, cache_k, cache_v, page_table, norm_mix_g, norm_ffn_g, a_w_in, a_b_in, a_w_dw, a_b_dw, a_ln_g, a_ln_b, a_w_out, a_b_out, kv_norm_g, w_k, w_v, b_w_q, b_w_o, rel_bias, f_w_gate, f_w_up, f_w_down, final_norm_g):
    B, S, D = x_prompt.shape
    DB, dec_seq, _ = x_sample.shape
    assert dec_seq == 1, "decode path handles one new token per sequence"
    n_a = a_w_in.shape[0]
    depth = norm_mix_g.shape[0]
    F = f_w_gate.shape[-1]
    hd = D // N_HEADS
    ppb = MOBA_BLOCK // PAGE_SIZE
    n_pages = page_table.shape[1]

    bf = lambda w: w.astype(BF16)
    a_w_in_b, a_w_out_b = bf(a_w_in), bf(a_w_out)
    w_k_b, w_v_b, w_q_b, w_o_b = bf(w_k)[None], bf(w_v)[None], bf(b_w_q), bf(b_w_o)
    w_gate_b, w_up_b, w_down_b = bf(f_w_gate), bf(f_w_up), bf(f_w_down)

    xp = x_prompt.reshape(B * S, D)
    xs = jnp.pad(x_sample.reshape(DB, D), ((0, DEC_ROWS - DB), (0, 0)))

    def mm(x, x_side, ws, n_cols, *, residual=None, outs=((F32, None),), side_dtypes=(F32,), out_scale=None,
           side_scale=None, **kw):
        res_main, res_side = residual if residual is not None else (None, None)
        return fused_matmul(x, ws, n_cols, residual=res_main, outs=outs, out_scale=out_scale,
                            side=(x_side, res_side, side_dtypes, side_scale), tm=ROW_TILE, **kw)

    conv_p, conv_s = [], []
    bias_tables = relbias_tables(rel_bias)
    page_ids = page_table.reshape(-1)
    ids_per_layer = -(-page_ids.shape[0] // n_a)
    page_sum_parts = []
    for l in range(depth):
        if l < n_a:
            b_in = a_b_in[l].reshape(1, 2 * D)
            ids_l = page_ids[l * ids_per_layer:(l + 1) * ids_per_layer]
            ids_glu, ids_out, ids_up = jnp.split(ids_l, [ids_l.shape[0] // 4, ids_l.shape[0] * 3 // 8])
            u, u_s, sums = mm(xp, xs, [(a_w_in_b, l, 0), (a_w_in_b, l, D)], D, gain=norm_mix_g[l],
                              biases=[(b_in, 0), (b_in, D)], epilogue="glu", page_stream=(cache_k, ids_glu),
                              name="glu")
            page_sum_parts.append(sums[:ids_glu.shape[0]])
            u3 = u.reshape(B, S, D)
            left = jnp.zeros((B, CONV_WIDTH - 1, D), F32)
            c = conv_ln_silu(u3, left, a_w_dw[l], a_b_dw[l], a_ln_g[l], a_ln_b[l],
                             tt=CONV_TIME_TILE).reshape(B * S, D)
            conv_p.append(u3[:, S - (CONV_WIDTH - 1):, :])
            c_s = conv_step(state_conv[l], u_s[:DB], a_w_dw[l], a_b_dw[l], a_ln_g[l], a_ln_b[l])
            c_s = jnp.pad(c_s, ((0, DEC_ROWS - DB), (0, 0)))
            conv_s.append(jnp.concatenate([state_conv[l][:, 1:, :], u_s[:DB, None, :]], axis=1))
            xp, xs, sums = mm(c, c_s, [(a_w_out_b, l, 0)], D, biases=[(a_b_out[l].reshape(1, D), 0)],
                              residual=(xp, xs), page_stream=(cache_k, ids_out), name="convout")
            page_sum_parts.append(sums[:ids_out.shape[0]])
        else:
            j = l - n_a
            if l == n_a:
                kv_outs = ((F32, hd), (BF16, None))
                k_heads, k_b, k_s = mm(xp, xs, [(w_k_b, 0, 0)], D, gain=kv_norm_g, outs=kv_outs, name="kproj")
                v_heads, v_b, v_s = mm(xp, xs, [(w_v_b, 0, 0)], D, gain=kv_norm_g, outs=kv_outs, name="vproj")
                page_sums = jnp.concatenate(page_sum_parts).reshape(DB, n_pages // ppb, ppb * D)
            q, q_s = mm(xp, xs, [(w_q_b, j, 0)], D, gain=norm_mix_g[l], outs=((BF16, None),),
                        out_scale=hd ** -0.5 * LOG2E, name="qproj")
            o = moba_prompt_attn(q, k_b, v_b, bias_tables, n_seq=B, seq=S)
            picks = moba_gate_topk(q_s, page_sums, k_s, n_seq=DB, ppb=ppb)
            sel = picks[:, :MOBA_TOPK, :N_HEADS].transpose(0, 2, 1)
            o8 = moba_decode_attn(sel, page_table, rel_bias, q_s, k_s, v_s, cache_k, cache_v, n_seq=DB)
            o_s = jnp.pad(o8[:, 0, :].reshape(DB, D), ((0, DEC_ROWS - DB), (0, 0))).astype(BF16)
            xp, xs = mm(o, o_s, [(w_o_b, j, 0)], D, residual=(xp, xs), name="oproj")
        ids = ids_up if l < n_a else None
        up = mm(xp, xs, [(w_gate_b, l, 0), (w_up_b, l, 0)], F, gain=norm_ffn_g[l], epilogue="swiglu",
                outs=((BF16, None),), side_dtypes=(BF16,),
                page_stream=None if ids is None else (cache_k, ids), name="ffn_up")
        act, act_s = up[0], up[1]
        if ids is not None:
            page_sum_parts.append(up[2][:ids.shape[0]])
        xp, xs = mm(act, act_s, [(w_down_b, l, 0)], D, residual=(xp, xs), name="ffn_down")

    y_prompt = rmsnorm_rows(xp, final_norm_g, tm=ROW_TILE // 2).reshape(B, S, D)
    y_sample = rmsnorm_rows(xs, final_norm_g, tm=DEC_ROWS)[:DB].reshape(DB, 1, D)
    return (y_prompt, y_sample, jnp.stack(conv_p), jnp.stack(conv_s),
            k_heads.reshape(B, S, N_HEADS, hd), v_heads.reshape(B, S, N_HEADS, hd),
            k_s[:DB].reshape(DB, 1, N_HEADS, hd), v_s[:DB].reshape(DB, 1, N_HEADS, hd))
```

```python
import functools
import math

import numpy as np
import jax
import jax.numpy as jnp
from jax import lax
from jax.experimental import pallas as pl
from jax.experimental.pallas import tpu as pltpu

F32 = jnp.float32
BF16 = jnp.bfloat16

N_HEADS = 16
CONV_WIDTH = 31
MOBA_BLOCK = 256
MOBA_TOPK = 3
NUM_BUCKETS = 32
MAX_DISTANCE = 128
PAGE_SIZE = 128
EPS = 1e-6
NEG = -1e30
LOG2E = math.log2(math.e)

V7X_VMEM_BYTES = 64 * 1024 * 1024
V7X_LANES = 128
V7X_SUBLANES = 8
VMEM_CAP = V7X_VMEM_BYTES - 4 * 1024 * 1024


def _vmem_limit(est_bytes):
    return int(min(max(est_bytes + (8 << 20), 16 << 20), VMEM_CAP))


def _bucket_thresholds():
    max_exact = NUM_BUCKETS // 2
    n = np.arange(0, 4 * MAX_DISTANCE, dtype=np.int64)
    nf = np.maximum(n, max_exact).astype(np.float32)
    large = max_exact + (np.log(nf / np.float32(max_exact)) / np.float32(math.log(MAX_DISTANCE / max_exact))
                         * np.float32(NUM_BUCKETS - max_exact)).astype(np.int32)
    large = np.minimum(large, NUM_BUCKETS - 1)
    bucket = np.where(n < max_exact, n, large)
    assert np.all(np.diff(bucket) >= 0) and bucket[-1] == NUM_BUCKETS - 1
    return [int(np.argmax(bucket >= k)) for k in range(1, NUM_BUCKETS)]


_BUCKET_START = _bucket_thresholds()


def _bias_from_distance(dist, table_at):
    bias = jnp.full(dist.shape, table_at(0), F32)
    for k in range(1, NUM_BUCKETS):
        bias = jnp.where(dist >= _BUCKET_START[k - 1], table_at(k), bias)
    return bias


def _mm_body(*refs, has_norm, n_w, has_bias, has_res, has_side, epilogue, out_scales, n_outs, n_pages):
    it = iter(refs)
    if n_pages:
        next(it)
    x_refs = [next(it) for _ in range(1 + has_side)]
    g_ref = next(it) if has_norm else None
    w_refs = [next(it) for _ in range(n_w)]
    b_refs = [next(it) for _ in range(n_w)] if has_bias else []
    res_refs = [next(it) for _ in range(1 + has_side)] if has_res else [None, None]
    page_refs = [next(it) for _ in range(n_pages)]
    out_refs = [[next(it) for _ in range(n)] for n in n_outs]
    page_out = next(it) if n_pages else None
    if has_norm:
        xn_refs = [next(it) for _ in x_refs]

        @pl.when(pl.program_id(1) == 0)
        def _():
            for x_ref, xn_ref in zip(x_refs, xn_refs):
                x = x_ref[...]
                ms = jnp.mean(x * x, axis=-1, keepdims=True)
                xn_ref[...] = (x * lax.rsqrt(ms + EPS) * g_ref[...]).astype(xn_ref.dtype)

        x_refs = xn_refs

    def stream(x_ref, res_ref, outs, out_scale):
        xb = x_ref[...]
        accs = [jnp.dot(xb, w[...], preferred_element_type=F32) for w in w_refs]
        if has_bias:
            accs = [a + b[...] for a, b in zip(accs, b_refs)]
        if epilogue == "glu":
            y = accs[0] * jax.nn.sigmoid(accs[1])
        elif epilogue == "swiglu":
            y = jax.nn.silu(accs[0]) * accs[1]
        else:
            y = accs[0]
        if out_scale is not None:
            y = y * out_scale
        if has_res:
            y = res_ref[...] + y
        for o in outs:
            if len(o.shape) == 3:
                hd = o.shape[-1]
                for h in range(o.shape[1]):
                    o[:, h, :] = y[:, h * hd:(h + 1) * hd].astype(o.dtype)
            else:
                o[...] = y.astype(o.dtype)

    stream(x_refs[0], res_refs[0], out_refs[0], out_scales[0])
    if has_side:
        @pl.when(pl.program_id(0) == 0)
        def _():
            stream(x_refs[1], res_refs[1], out_refs[1], out_scales[1])

    for k, page_ref in enumerate(page_refs):
        page_out[k] = jnp.sum(page_ref[...], axis=0)


MATMUL_VMEM_BUDGET = 48 * 1024 * 1024


def fused_matmul(x, ws, n_cols, *, gain=None, biases=None, residual=None, epilogue="none",
                 outs=((F32, None),), out_scale=None, side=None, page_stream=None, tm, name):
    M, K = x.shape
    has_norm = gain is not None
    has_side = side is not None
    out_bytes = sum(jnp.dtype(dt).itemsize for dt, _ in outs) + (4 if residual is not None else 0)

    def pages_per_step(tn):
        return -(-page_stream[1].shape[0] // ((M // tm) * (n_cols // tn))) if page_stream is not None else 0

    def estimate(tn):
        page_bytes = 2 * pages_per_step(tn) * math.prod(page_stream[0].shape[1:]) * 4 if page_stream else 0
        return (2 * tm * K * x.dtype.itemsize + (tm * K * 2 if has_norm else 0) + 2 * len(ws) * K * tn * 2
                + 2 * tm * tn * out_bytes + (len(ws) + 1) * tm * tn * 4 + page_bytes)

    def legal(tn):
        heads_ok = all(hd is None or (tn // hd) % V7X_SUBLANES == 0 or tn == n_cols for _, hd in outs)
        offs_ok = all(off % tn == 0 for _, _, off in ws)
        return n_cols % tn == 0 and heads_ok and offs_ok

    tiles = [t for t in (2048, 1024, 512, 256, 128) if legal(t)]
    fitting = [t for t in tiles if estimate(t) <= MATMUL_VMEM_BUDGET]
    tn = fitting[0] if fitting else tiles[-1]
    assert M % tm == 0
    nj = n_cols // tn
    grid = (M // tm, nj)
    in_specs = [pl.BlockSpec((tm, K), lambda i, j, *_: (i, 0))]
    args = [x]
    if has_side:
        x_side, res_side, side_dtypes, side_scale = side
        R = x_side.shape[0]
        side_block = lambda i, j, *_: (0, jnp.where(i == 0, j, nj - 1))
        in_specs.append(pl.BlockSpec((R, K), lambda i, j, *_: (0, 0)))
        args.append(x_side)
    if has_norm:
        in_specs.append(pl.BlockSpec((1, K), lambda i, j, *_: (0, 0)))
        args.append(gain.reshape(1, K))
    for w, layer, off in ws:
        in_specs.append(pl.BlockSpec((None, K, tn), lambda i, j, *_, layer=layer, ob=off // tn: (layer, 0, j + ob)))
        args.append(w)
    if biases is not None:
        for b, off in biases:
            in_specs.append(pl.BlockSpec((1, tn), lambda i, j, *_, ob=off // tn: (0, j + ob)))
            args.append(b)
    if residual is not None:
        in_specs.append(pl.BlockSpec((tm, tn), lambda i, j, *_: (i, j)))
        args.append(residual)
        if has_side:
            in_specs.append(pl.BlockSpec((R, tn), side_block))
            args.append(res_side)
    out_specs, out_shape = [], []
    for dt, hd in outs:
        if hd is None:
            out_specs.append(pl.BlockSpec((tm, tn), lambda i, j, *_: (i, j)))
            out_shape.append(jax.ShapeDtypeStruct((M, n_cols), dt))
        else:
            out_specs.append(pl.BlockSpec((tm, tn // hd, hd), lambda i, j, *_: (i, j, 0)))
            out_shape.append(jax.ShapeDtypeStruct((M, n_cols // hd, hd), dt))
    n_outs, out_scales = [len(outs)], [out_scale]
    if has_side:
        for dt in side_dtypes:
            out_specs.append(pl.BlockSpec((R, tn), side_block))
            out_shape.append(jax.ShapeDtypeStruct((R, n_cols), dt))
        n_outs.append(len(side_dtypes))
        out_scales.append(side_scale)
    prefetch = []
    per_step = pages_per_step(tn)
    if page_stream is not None:
        cache, ids = page_stream
        n_slots = grid[0] * nj * per_step
        assert ids.shape[0] <= n_slots
        prefetch = [jnp.concatenate([ids, jnp.broadcast_to(ids[-1:], (n_slots - ids.shape[0],))])]
        for k in range(per_step):
            in_specs.append(pl.BlockSpec(
                (None,) + cache.shape[1:],
                lambda i, j, ids_ref, k=k: (ids_ref[(i * nj + j) * per_step + k], 0, 0, 0)))
            args.append(cache)
        out_specs.append(pl.BlockSpec((per_step,) + cache.shape[2:], lambda i, j, *_: (i * nj + j, 0, 0)))
        out_shape.append(jax.ShapeDtypeStruct((n_slots,) + cache.shape[2:], F32))
    scratch = []
    if has_norm:
        scratch = [pltpu.VMEM((tm, K), BF16)] + ([pltpu.VMEM((R, K), BF16)] if has_side else [])
    body = functools.partial(_mm_body, has_norm=has_norm, n_w=len(ws), has_bias=biases is not None,
                             has_res=residual is not None, has_side=has_side, epilogue=epilogue,
                             out_scales=out_scales, n_outs=n_outs, n_pages=per_step)
    res = pl.pallas_call(
        body,
        grid_spec=pltpu.PrefetchScalarGridSpec(num_scalar_prefetch=len(prefetch), grid=grid, in_specs=in_specs,
                                               out_specs=out_specs, scratch_shapes=scratch),
        out_shape=out_shape, name=name,
        compiler_params=pltpu.CompilerParams(dimension_semantics=("arbitrary", "arbitrary"),
                                             vmem_limit_bytes=_vmem_limit(estimate(tn))),
    )(*prefetch, *args)
    return res if len(res) > 1 else res[0]


def _rmsnorm_body(x_ref, g_ref, o_ref):
    x = x_ref[...]
    ms = jnp.mean(x * x, axis=-1, keepdims=True)
    o_ref[...] = x * lax.rsqrt(ms + EPS) * g_ref[...]


def rmsnorm_rows(x, gain, *, tm):
    M, D = x.shape
    return pl.pallas_call(
        _rmsnorm_body, grid=(M // tm,),
        in_specs=[pl.BlockSpec((tm, D), lambda i: (i, 0)), pl.BlockSpec((1, D), lambda i: (0, 0))],
        out_specs=pl.BlockSpec((tm, D), lambda i: (i, 0)),
        out_shape=jax.ShapeDtypeStruct((M, D), F32), name="rmsnorm_rows",
        compiler_params=pltpu.CompilerParams(dimension_semantics=("arbitrary",),
                                             vmem_limit_bytes=_vmem_limit(4 * tm * D * 4)),
    )(x, gain.reshape(1, D))


CONV_HALO = 32
CONV_ROWS = 64


def _ln_silu(c, g, b):
    mu = jnp.mean(c, axis=-1, keepdims=True)
    xc = c - mu
    var = jnp.mean(xc * xc, axis=-1, keepdims=True)
    y = xc * lax.rsqrt(var + EPS) * g + b
    return y * jax.nn.sigmoid(y)


def _conv_body(u_ref, halo_ref, left_ref, w_ref, bdw_ref, g_ref, b_ref, o_ref, ext_ref, y_ref, *, tt):
    first = pl.program_id(1) == 0
    ext_ref[0:CONV_HALO, :] = jnp.where(first, left_ref[...], halo_ref[...])
    ext_ref[CONV_HALO:, :] = u_ref[...]
    d_model = u_ref.shape[-1]
    lead = CONV_HALO - (CONV_WIDTH - 1)

    def chunk(c, carry):
        lanes = pl.ds(pl.multiple_of(c * V7X_LANES, V7X_LANES), V7X_LANES)
        w = w_ref[:, lanes]
        for r0 in range(0, tt, CONV_ROWS):
            n_win = CONV_ROWS + CONV_HALO
            win = ext_ref[r0:r0 + n_win, lanes]
            acc = jnp.zeros((CONV_ROWS, V7X_LANES), F32)
            for res in range(V7X_SUBLANES):
                shifted = win if res == 0 else pltpu.roll(win, n_win - res, axis=0)
                for off in range(res, lead + CONV_WIDTH, V7X_SUBLANES):
                    k = off - lead
                    if k >= 0:
                        acc = acc + shifted[off - res:off - res + CONV_ROWS, :] * w[k:k + 1, :]
            y_ref[r0:r0 + CONV_ROWS, lanes] = acc
        return carry

    lax.fori_loop(0, d_model // V7X_LANES, chunk, 0)
    c = y_ref[...] + bdw_ref[...]
    o_ref[...] = _ln_silu(c, g_ref[...], b_ref[...]).astype(o_ref.dtype)


def conv_ln_silu(u, left, w_dw, b_dw, ln_g, ln_b, *, tt):
    B, T, D = u.shape
    left_pad = jnp.pad(left, ((0, 0), (CONV_HALO - (CONV_WIDTH - 1), 0), (0, 0)))
    hb = tt // CONV_HALO
    row = lambda a: a.reshape(1, D)
    est = 2 * tt * D * 4 + 4 * CONV_HALO * D * 4 + (2 * tt + CONV_HALO) * D * 4 + 2 * tt * D * 2 + 6 * tt * D * 4
    return pl.pallas_call(
        functools.partial(_conv_body, tt=tt), grid=(B, T // tt),
        in_specs=[
            pl.BlockSpec((None, tt, D), lambda b, i: (b, i, 0)),
            pl.BlockSpec((None, CONV_HALO, D), lambda b, i: (b, jnp.maximum(i * hb - 1, 0), 0)),
            pl.BlockSpec((None, CONV_HALO, D), lambda b, i: (b, 0, 0)),
            pl.BlockSpec((CONV_WIDTH, D), lambda b, i: (0, 0)),
            pl.BlockSpec((1, D), lambda b, i: (0, 0)),
            pl.BlockSpec((1, D), lambda b, i: (0, 0)),
            pl.BlockSpec((1, D), lambda b, i: (0, 0)),
        ],
        out_specs=pl.BlockSpec((None, tt, D), lambda b, i: (b, i, 0)),
        out_shape=jax.ShapeDtypeStruct((B, T, D), BF16),
        scratch_shapes=[pltpu.VMEM((tt + CONV_HALO, D), F32), pltpu.VMEM((tt, D), F32)],
        name="conv_ln_silu",
        compiler_params=pltpu.CompilerParams(dimension_semantics=("arbitrary", "arbitrary"),
                                             vmem_limit_bytes=_vmem_limit(est)),
    )(u, u, left_pad, w_dw, row(b_dw), row(ln_g), row(ln_b))


def _conv_step_body(state_ref, u_ref, w_ref, bdw_ref, g_ref, b_ref, o_ref):
    acc = u_ref[...] * w_ref[CONV_WIDTH - 1:CONV_WIDTH, :]
    for k in range(CONV_WIDTH - 1):
        acc = acc + state_ref[:, k, :] * w_ref[k:k + 1, :]
    c = acc + bdw_ref[...]
    o_ref[...] = _ln_silu(c, g_ref[...], b_ref[...]).astype(o_ref.dtype)


def conv_step(state, u_new, w_dw, b_dw, ln_g, ln_b):
    NB, _, D = state.shape
    row = lambda a: a.reshape(1, D)
    return pl.pallas_call(
        _conv_step_body,
        out_shape=jax.ShapeDtypeStruct((NB, D), BF16), name="conv_step",
    )(state, u_new, w_dw, row(b_dw), row(ln_g), row(ln_b))


def _relbias_body(table_ref, o_ref):
    h = pl.program_id(0)
    a = lax.broadcasted_iota(jnp.int32, (MOBA_BLOCK, MOBA_BLOCK), 0)
    b = lax.broadcasted_iota(jnp.int32, (MOBA_BLOCK, MOBA_BLOCK), 1)
    far = table_ref[NUM_BUCKETS - 1, h]
    table_at = lambda k: table_ref[k, h]
    d_own = a - b
    own = (_bias_from_distance(d_own, table_at) - far) * LOG2E
    o_ref[0] = jnp.where(d_own >= 0, own, NEG)
    o_ref[1] = (_bias_from_distance(d_own + MOBA_BLOCK, table_at) - far) * LOG2E


def relbias_tables(rel_bias):
    return pl.pallas_call(
        _relbias_body, grid=(N_HEADS,),
        in_specs=[pl.BlockSpec(memory_space=pltpu.SMEM)],
        out_specs=pl.BlockSpec((None, 2, MOBA_BLOCK, MOBA_BLOCK), lambda h: (h, 0, 0, 0)),
        out_shape=jax.ShapeDtypeStruct((N_HEADS, 2, MOBA_BLOCK, MOBA_BLOCK), F32), name="relbias_tables",
    )(rel_bias)


GATE_ROWS = 16


def _moba_prompt_body(q_ref, k_ref, v_ref, bias_ref, o_ref, kx_ref, vx_ref, s_ref, p_ref, acc_ref, *, nb):
    blk = MOBA_BLOCK
    hd = q_ref.shape[-1]
    seq = nb * blk
    nt = (((1,), (1,)), ((), ()))

    k = k_ref[...]
    kx_ref[:, 0:hd] = k
    kr = lax.broadcasted_iota(jnp.int32, (seq, V7X_LANES), 0)
    kc = lax.broadcasted_iota(jnp.int32, (seq, V7X_LANES), 1)
    kx_ref[:, hd:] = (kr // blk == kc).astype(BF16)
    vx_ref[:, 0:hd] = v_ref[...]
    vx_ref[:, hd:] = (kc == 0).astype(BF16)
    r = lax.broadcasted_iota(jnp.int32, (GATE_ROWS, seq), 0)
    c = lax.broadcasted_iota(jnp.int32, (GATE_ROWS, seq), 1)
    avg = jnp.where(c // blk == r, 1.0 / blk, 0.0).astype(BF16)
    km = jnp.dot(avg, k, preferred_element_type=F32)
    km_hi = km.astype(BF16)
    km_lo = (km - km_hi.astype(F32)).astype(BF16)

    q = q_ref[...]
    g2 = (lax.dot_general(km_hi, q, nt, preferred_element_type=F32)
          + lax.dot_general(km_lo, q, nt, preferred_element_type=F32))
    gate = g2[0:V7X_SUBLANES]
    row = lax.broadcasted_iota(jnp.int32, gate.shape, 0)
    own_blk = lax.broadcasted_iota(jnp.int32, gate.shape, 1) // blk
    gate = jnp.where(row < own_blk, gate, NEG)
    beaten = jnp.zeros(gate.shape, jnp.int32)
    for jp in range(nb):
        gj = gate[jp:jp + 1, :]
        beaten = beaten + ((gj > gate) | ((gj == gate) & (jp < row))).astype(jnp.int32)
    chosen = (beaten < MOBA_TOPK) & (row < own_blk)
    mask_t = jnp.where(chosen | (row == own_blk), 0.0, NEG)
    mask_t = jnp.concatenate([mask_t, jnp.zeros((V7X_LANES - V7X_SUBLANES, seq), F32)], axis=0)
    qx = jnp.concatenate([q, mask_t.T.astype(BF16)], axis=1)

    for j in range(nb):
        s_ref[j * blk:, j * blk:(j + 1) * blk] = lax.dot_general(
            qx[j * blk:, :], kx_ref[j * blk:(j + 1) * blk, :], nt, preferred_element_type=F32)
    for own in range(nb):
        n = (own + 1) * blk
        rows = slice(own * blk, n)
        parts = [s_ref[rows, n - blk:n] + bias_ref[0]]
        if own >= 1:
            parts.insert(0, s_ref[rows, n - 2 * blk:n - blk] + bias_ref[1])
        if own >= 2:
            parts.insert(0, s_ref[rows, 0:n - 2 * blk])
        s = jnp.concatenate(parts, axis=1) if len(parts) > 1 else parts[0]
        m = jnp.max(s, axis=-1, keepdims=True)
        p_ref[rows, 0:n] = jnp.exp2(s - m).astype(BF16)
    for j in range(nb):
        pv = jnp.dot(p_ref[j * blk:, j * blk:(j + 1) * blk], vx_ref[j * blk:(j + 1) * blk, :],
                     preferred_element_type=F32)
        if j == 0:
            acc_ref[...] = pv
        else:
            acc_ref[j * blk:, :] += pv
    acc = acc_ref[...]
    o_ref[...] = (acc[:, 0:hd] * (1.0 / acc[:, hd:hd + 1])).astype(o_ref.dtype)


def moba_prompt_attn(q, k, v, bias, *, n_seq, seq):
    M, HD = q.shape
    hd = HD // N_HEADS
    nb = seq // MOBA_BLOCK
    assert seq % MOBA_BLOCK == 0 and nb <= V7X_SUBLANES
    head_spec = pl.BlockSpec((seq, hd), lambda b, h: (b, h))
    return pl.pallas_call(
        functools.partial(_moba_prompt_body, nb=nb), grid=(n_seq, N_HEADS),
        in_specs=[head_spec, head_spec, head_spec,
                  pl.BlockSpec((None, 2, MOBA_BLOCK, MOBA_BLOCK), lambda b, h: (h, 0, 0, 0))],
        out_specs=head_spec,
        out_shape=jax.ShapeDtypeStruct((M, HD), BF16),
        scratch_shapes=[pltpu.VMEM((seq, hd + V7X_LANES), BF16), pltpu.VMEM((seq, hd + V7X_LANES), BF16),
                        pltpu.VMEM((seq, seq), F32), pltpu.VMEM((seq, seq), BF16),
                        pltpu.VMEM((seq, hd + V7X_LANES), F32)],
        name="moba_prompt_attn",
        compiler_params=pltpu.CompilerParams(dimension_semantics=("arbitrary", "arbitrary"),
                                             vmem_limit_bytes=_vmem_limit(seq * seq * 6 + 16 * seq * hd * 4)),
    )(q, k, v, bias)


def _gate_topk_body(q_ref, sums_ref, knew_ref, o_ref, *, nbp, ppb, own, inv_rows):
    b = pl.program_id(0)
    HD = q_ref.shape[-1]
    hd = HD // N_HEADS
    q = q_ref[pl.ds(b, 1), :]
    r = lax.broadcasted_iota(jnp.int32, (HD, V7X_LANES), 0)
    c = lax.broadcasted_iota(jnp.int32, (HD, V7X_LANES), 1)
    ind = (r // hd == c).astype(BF16)

    def head_sums(prod):
        hi = prod.astype(BF16)
        mid = (prod - hi.astype(F32)).astype(BF16)
        lo = (prod - hi.astype(F32) - mid.astype(F32)).astype(BF16)
        return (jnp.dot(hi, ind, preferred_element_type=F32) + jnp.dot(mid, ind, preferred_element_type=F32)
                + jnp.dot(lo, ind, preferred_element_type=F32))

    block_sums = sums_ref[:, 0:HD]
    for p in range(1, ppb):
        block_sums = block_sums + sums_ref[:, p * HD:(p + 1) * HD]
    means = block_sums * inv_rows
    gate_past = head_sums(means * q)
    k_new = jnp.broadcast_to(knew_ref[pl.ds(b, 1), :], (V7X_SUBLANES, HD))
    gate_rec = head_sums((k_new * inv_rows) * q)
    gate = jnp.concatenate([gate_past, gate_rec], axis=0)
    idx = lax.broadcasted_iota(jnp.int32, gate.shape, 0)
    gate = jnp.where(idx < own, gate, NEG)
    gate = jnp.where(idx <= nbp, gate, -jnp.inf)
    picks = []
    for _ in range(MOBA_TOPK):
        mx = jnp.max(gate, axis=0, keepdims=True)
        pick = jnp.min(jnp.where(gate == mx, idx, nbp + V7X_SUBLANES), axis=0, keepdims=True)
        picks.append(pick)
        gate = jnp.where(idx == pick, -jnp.inf, gate)
    picks.append(jnp.zeros((V7X_SUBLANES - MOBA_TOPK, V7X_LANES), jnp.int32))
    o_ref[...] = jnp.concatenate(picks, axis=0)


def moba_gate_topk(q_rows, page_sums, k_new_rows, *, n_seq, ppb):
    R, HD = q_rows.shape
    _, nbp, _ = page_sums.shape
    return pl.pallas_call(
        functools.partial(_gate_topk_body, nbp=nbp, ppb=ppb, own=nbp, inv_rows=1.0 / MOBA_BLOCK), grid=(n_seq,),
        in_specs=[
            pl.BlockSpec((R, HD), lambda b: (0, 0)),
            pl.BlockSpec((None, nbp, ppb * HD), lambda b: (b, 0, 0)),
            pl.BlockSpec((R, HD), lambda b: (0, 0)),
        ],
        out_specs=pl.BlockSpec((None, V7X_SUBLANES, V7X_LANES), lambda b: (b, 0, 0)),
        out_shape=jax.ShapeDtypeStruct((n_seq, V7X_SUBLANES, V7X_LANES), jnp.int32), name="moba_gate_topk",
        compiler_params=pltpu.CompilerParams(dimension_semantics=("arbitrary",)),
    )(q_rows, page_sums, k_new_rows)


DEC_HEADS_PER_STEP = 8


def _decode_attn_body(sel_ref, pt_ref, table_ref, q_ref, kn_ref, vn_ref, ck_ref, cv_ref, o_ref,
                      kbuf, vbuf, sem, *, nbp, ppb, t_new, scale):
    step = pl.program_id(0)
    hd = kbuf.shape[-1]
    hg = DEC_HEADS_PER_STEP
    groups = N_HEADS // hg
    n_buf = MOBA_TOPK * ppb

    def page_copies(st, slot):
        b = st // groups
        out = []
        for hh in range(hg):
            h = (st % groups) * hg + hh
            for kk in range(MOBA_TOPK):
                blk = jnp.minimum(sel_ref[b, h, kk], nbp - 1)
                for p in range(ppb):
                    page = pt_ref[b, blk * ppb + p]
                    i = hh * n_buf + kk * ppb + p
                    out.append(pltpu.make_async_copy(ck_ref.at[page, :, h, :], kbuf.at[slot, i], sem.at[0, slot, i]))
                    out.append(pltpu.make_async_copy(cv_ref.at[page, :, h, :], vbuf.at[slot, i], sem.at[1, slot, i]))
        return out

    slot = step % 2

    @pl.when(step == 0)
    def _():
        for n, c in enumerate(page_copies(step, slot)):
            c.start(priority=n % 2)

    @pl.when(step + 1 < pl.num_programs(0))
    def _():
        for n, c in enumerate(page_copies(step + 1, 1 - slot)):
            c.start(priority=n % 2)

    for c in page_copies(step, slot):
        c.wait()

    b = step // groups
    nt = (((1,), (1,)), ((), ()))
    for hh in range(hg):
        h = (step % groups) * hg + hh
        lanes = pl.ds(pl.multiple_of(h * hd, hd), hd)

        def rows8(ref):
            x = ref[:, lanes]
            r = lax.broadcasted_iota(jnp.int32, x.shape, 0)
            row = jnp.sum(jnp.where(r == b, x, 0.0), axis=0, keepdims=True)
            return jnp.broadcast_to(row, (V7X_SUBLANES, hd))

        q = rows8(q_ref)
        qb = q.astype(BF16)
        table_at = lambda k, h=h: table_ref[k, h]
        kn, vn = rows8(kn_ref), rows8(vn_ref)
        l_new = jnp.sum(q * kn, axis=-1, keepdims=True) * scale + table_at(0)
        logits, values = [], []
        for kk in range(MOBA_TOPK):
            sel = sel_ref[b, h, kk]
            valid = sel < nbp
            selp = jnp.minimum(sel, nbp - 1)
            for p in range(ppb):
                i = hh * n_buf + kk * ppb + p
                kp = kbuf[slot, i].astype(BF16)
                s = lax.dot_general(qb, kp, nt, preferred_element_type=F32) * scale
                pos = selp * MOBA_BLOCK + p * PAGE_SIZE + lax.broadcasted_iota(jnp.int32, s.shape, 1)
                s = s + _bias_from_distance(t_new - pos, table_at)
                logits.append(jnp.where(valid, s, NEG))
                values.append(vbuf[slot, i].astype(BF16))
        m = l_new
        for s in logits:
            m = jnp.maximum(m, jnp.max(s, axis=-1, keepdims=True))
        p_new = jnp.exp(l_new - m)
        den = p_new
        acc = p_new * vn
        for s, v in zip(logits, values):
            p = jnp.exp(s - m)
            den = den + jnp.sum(p, axis=-1, keepdims=True)
            acc = acc + jnp.dot(p.astype(BF16), v, preferred_element_type=F32)
        o_ref[hh] = acc / den


def moba_decode_attn(sel, page_table, rel_bias, q_rows, k_new_rows, v_new_rows, cache_k, cache_v, *, n_seq):
    R, HD = q_rows.shape
    hd = HD // N_HEADS
    n_pages = page_table.shape[1]
    ppb = MOBA_BLOCK // PAGE_SIZE
    nbp = n_pages // ppb
    hg = DEC_HEADS_PER_STEP
    assert n_pages % ppb == 0, "past length must be a whole number of MoBA blocks"
    assert N_HEADS % hg == 0
    row_spec = pl.BlockSpec((R, HD), lambda s, sel, pt: (0, 0))
    n_buf = hg * MOBA_TOPK * ppb
    body = functools.partial(_decode_attn_body, nbp=nbp, ppb=ppb, t_new=n_pages * PAGE_SIZE, scale=hd ** -0.5)
    return pl.pallas_call(
        body,
        grid_spec=pltpu.PrefetchScalarGridSpec(
            num_scalar_prefetch=2, grid=(n_seq * N_HEADS // hg,),
            in_specs=[pl.BlockSpec(memory_space=pltpu.SMEM), row_spec, row_spec, row_spec,
                      pl.BlockSpec(memory_space=pl.ANY), pl.BlockSpec(memory_space=pl.ANY)],
            out_specs=pl.BlockSpec((hg, V7X_SUBLANES, hd), lambda s, sel, pt: (s, 0, 0)),
            scratch_shapes=[pltpu.VMEM((2, n_buf, PAGE_SIZE, hd), F32), pltpu.VMEM((2, n_buf, PAGE_SIZE, hd), F32),
                            pltpu.SemaphoreType.DMA((2, 2, n_buf))],
        ),
        out_shape=jax.ShapeDtypeStruct((n_seq * N_HEADS, V7X_SUBLANES, hd), F32), name="moba_decode_attn",
        compiler_params=pltpu.CompilerParams(dimension_semantics=("arbitrary",)),
    )(sel, page_table, rel_bias, q_rows, k_new_rows, v_new_rows, cache_k, cache_v)


ROW_TILE = 1024
DEC_ROWS = 16
CONV_TIME_TILE = 512


def kernel(x_prompt, x_sample, state_conv, cache_k, cache_v, page_table, norm_mix_g, norm_ffn_g, a_w_in, a_b_in, a_w_dw, a_b_dw, a_ln_g, a_ln_b, a_w_out, a_b_out, kv_norm_g, w_k, w_v, b_w_q, b_w_o, rel_bias, f_w_gate, f_w_up, f_w_down, final_norm_g):
    B, S, D = x_prompt.shape
    DB, dec_seq, _ = x_sample.shape
    assert dec_seq == 1, "decode path handles one new token per sequence"
    n_a = a_w_in.shape[0]
    depth = norm_mix_g.shape[0]
    F = f_w_gate.shape[-1]
    hd = D // N_HEADS
    ppb = MOBA_BLOCK // PAGE_SIZE
    n_pages = page_table.shape[1]

    bf = lambda w: w.astype(BF16)
    a_w_in_b, a_w_out_b = bf(a_w_in), bf(a_w_out)
    w_k_b, w_v_b, w_q_b, w_o_b = bf(w_k)[None], bf(w_v)[None], bf(b_w_q), bf(b_w_o)
    w_gate_b, w_up_b, w_down_b = bf(f_w_gate), bf(f_w_up), bf(f_w_down)

    xp = x_prompt.reshape(B * S, D)
    xs = jnp.pad(x_sample.reshape(DB, D), ((0, DEC_ROWS - DB), (0, 0)))

    def mm(x, x_side, ws, n_cols, *, residual=None, outs=((F32, None),), side_dtypes=(F32,), out_scale=None,
           side_scale=None, **kw):
        res_main, res_side = residual if residual is not None else (None, None)
        return fused_matmul(x, ws, n_cols, residual=res_main, outs=outs, out_scale=out_scale,
                            side=(x_side, res_side, side_dtypes, side_scale), tm=ROW_TILE, **kw)

    conv_p, conv_s = [], []
    bias_tables = relbias_tables(rel_bias)
    page_ids = page_table.reshape(-1)
    ids_per_layer = -(-page_ids.shape[0] // n_a)
    page_sum_parts = []
    for l in range(depth):
        if l < n_a:
            b_in = a_b_in[l].reshape(1, 2 * D)
            u, u_s = mm(xp, xs, [(a_w_in_b, l, 0), (a_w_in_b, l, D)], D, gain=norm_mix_g[l],
                        biases=[(b_in, 0), (b_in, D)], epilogue="glu", name="glu")
            u3 = u.reshape(B, S, D)
            left = jnp.zeros((B, CONV_WIDTH - 1, D), F32)
            c = conv_ln_silu(u3, left, a_w_dw[l], a_b_dw[l], a_ln_g[l], a_ln_b[l],
                             tt=CONV_TIME_TILE).reshape(B * S, D)
            conv_p.append(u3[:, S - (CONV_WIDTH - 1):, :])
            c_s = conv_step(state_conv[l], u_s[:DB], a_w_dw[l], a_b_dw[l], a_ln_g[l], a_ln_b[l])
            c_s = jnp.pad(c_s, ((0, DEC_ROWS - DB), (0, 0)))
            conv_s.append(jnp.concatenate([state_conv[l][:, 1:, :], u_s[:DB, None, :]], axis=1))
            xp, xs = mm(c, c_s, [(a_w_out_b, l, 0)], D, biases=[(a_b_out[l].reshape(1, D), 0)],
                        residual=(xp, xs), name="convout")
        else:
            j = l - n_a
            if l == n_a:
                kv_outs = ((F32, hd), (BF16, None))
                k_heads, k_b, k_s = mm(xp, xs, [(w_k_b, 0, 0)], D, gain=kv_norm_g, outs=kv_outs, name="kproj")
                v_heads, v_b, v_s = mm(xp, xs, [(w_v_b, 0, 0)], D, gain=kv_norm_g, outs=kv_outs, name="vproj")
                page_sums = jnp.concatenate(page_sum_parts).reshape(DB, n_pages // ppb, ppb * D)
            q, q_s = mm(xp, xs, [(w_q_b, j, 0)], D, gain=norm_mix_g[l], outs=((BF16, None),),
                        out_scale=hd ** -0.5 * LOG2E, name="qproj")
            o = moba_prompt_attn(q, k_b, v_b, bias_tables, n_seq=B, seq=S)
            picks = moba_gate_topk(q_s, page_sums, k_s, n_seq=DB, ppb=ppb)
            sel = picks[:, :MOBA_TOPK, :N_HEADS].transpose(0, 2, 1)
            o8 = moba_decode_attn(sel, page_table, rel_bias, q_s, k_s, v_s, cache_k, cache_v, n_seq=DB)
            o_s = jnp.pad(o8[:, 0, :].reshape(DB, D), ((0, DEC_ROWS - DB), (0, 0))).astype(BF16)
            xp, xs = mm(o, o_s, [(w_o_b, j, 0)], D, residual=(xp, xs), name="oproj")
        ids = page_ids[l * ids_per_layer:(l + 1) * ids_per_layer] if l < n_a else None
        up = mm(xp, xs, [(w_gate_b, l, 0), (w_up_b, l, 0)], F, gain=norm_ffn_g[l], epilogue="swiglu",
                outs=((BF16, None),), side_dtypes=(BF16,),
                page_stream=None if ids is None else (cache_k, ids), name="ffn_up")
        act, act_s = up[0], up[1]
        if ids is not None:
            page_sum_parts.append(up[2][:ids.shape[0]])
        xp, xs = mm(act, act_s, [(w_down_b, l, 0)], D, residual=(xp, xs), name="ffn_down")

    y_prompt = rmsnorm_rows(xp, final_norm_g, tm=ROW_TILE // 2).reshape(B, S, D)
    y_sample = rmsnorm_rows(xs, final_norm_g, tm=DEC_ROWS)[:DB].reshape(DB, 1, D)
    return (y_prompt, y_sample, jnp.stack(conv_p), jnp.stack(conv_s),
            k_heads.reshape(B, S, N_HEADS, hd), v_heads.reshape(B, S, N_HEADS, hd),
            k_s[:DB].reshape(DB, 1, N_HEADS, hd), v_s[:DB].reshape(DB, 1, N_HEADS, hd))
```
